```python
import jax, jax.numpy as jnp
from jax import lax
import numpy as np

D_MODEL = 1024
BATCH = 8
SEQ = 2048
DEPTH = 2
DEC_BATCH = 128
DEC_SEQ = 4
PAST_LEN = 16384
PAGE_SIZE = 128

HEAD_DIM = 64
D_MIX = D_MODEL
N_HEADS_A = 6
N_HEADS_B = 6
N_GROUPS_C = 4
D_A = N_HEADS_A * HEAD_DIM
D_B = N_HEADS_B * HEAD_DIM
D_C = D_MIX - D_A - D_B
C_GROUP = D_C // N_GROUPS_C
POOL_WINDOWS = (2, 4, 8, 16)
POOL_BUF = max(POOL_WINDOWS) - 1
CONV_A = 3
CONV_B = 31
D_IN = 3 * D_A + 2 * D_B + D_C
IN_SPLITS = (D_A, 2 * D_A, 3 * D_A, 3 * D_A + D_B, 3 * D_A + 2 * D_B)
N_EXPERTS = 64
TOP_K = 8
N_EXPERT_GROUPS = 8
TOPK_GROUPS = 4
D_EXPERT = 256
D_SHARED = 256
ROUTED_SCALE = 2.5
N_MOD = 6
EPS = 1e-6

kernel_name = 'hybrid_conv_pool_moe_adaln_decode_step'


def rms_norm(x, g):
    xf = x.astype(jnp.float32)
    y = xf * lax.rsqrt(jnp.mean(xf * xf, axis=-1, keepdims=True) + EPS)
    return (y * g.astype(jnp.float32)).astype(x.dtype)


def layer_norm(x, g, b):
    xf = x.astype(jnp.float32)
    mu = jnp.mean(xf, axis=-1, keepdims=True)
    var = jnp.mean(jnp.square(xf - mu), axis=-1, keepdims=True)
    y = (xf - mu) * lax.rsqrt(var + EPS)
    return (y * g.astype(jnp.float32) + b.astype(jnp.float32)).astype(x.dtype)


def causal_depthwise_conv(buf, u, w):
    xx = jnp.concatenate([buf.astype(u.dtype), u], axis=1)
    k = w.shape[0]
    out = lax.conv_general_dilated(xx, w[:, None, :].astype(u.dtype), window_strides=(1,), padding='VALID',
                                   dimension_numbers=('NWC', 'WIO', 'NWC'), feature_group_count=u.shape[-1])
    return out, xx[:, xx.shape[1] - (k - 1):, :]


def short_conv_mixer(xa, ca, ba, buf, w):
    v, new_buf = causal_depthwise_conv(buf, ca * xa, w)
    return ba * v, new_buf


def conformer_conv_mixer(a, g, buf, w, bias, ln_g, ln_b):
    u = a * jax.nn.sigmoid(g)
    v, new_buf = causal_depthwise_conv(buf, u, w)
    v = layer_norm(v + bias.astype(v.dtype), ln_g, ln_b)
    return jax.nn.silu(v), new_buf


def multiscale_pool_mixer(p, buf, pos, w_pool, scale):
    n, t, _ = p.shape
    xx = jnp.concatenate([buf.astype(p.dtype), p], axis=1)
    xf = xx.astype(jnp.float32)
    cs = jnp.concatenate([jnp.zeros_like(xf[:, :1]), jnp.cumsum(xf, axis=1)], axis=1)
    cur = cs[:, POOL_BUF + 1:]
    pooled = []
    for gi, win in enumerate(POOL_WINDOWS):
        lo, hi = gi * C_GROUP, (gi + 1) * C_GROUP
        start = POOL_BUF + 1 - win
        wsum = cur[..., lo:hi] - cs[:, start:start + t, lo:hi]
        cnt = jnp.minimum(pos + 1, win).astype(jnp.float32)[None, :, None]
        pooled.append(wsum / cnt)
    d = jnp.concatenate(pooled, axis=-1) - p.astype(jnp.float32)
    d = d.reshape(n, t, N_GROUPS_C, C_GROUP)
    y = jnp.einsum('btgc,gcd->btgd', d, w_pool.astype(jnp.float32)).reshape(n, t, D_C)
    return (y * scale.astype(jnp.float32)).astype(p.dtype), xx[:, xx.shape[1] - POOL_BUF:, :]


def moe_ffn(h, w_router, router_bias, w_gate, w_up, w_down, ws_gate, ws_up, ws_down):
    lead = h.shape[:-1]
    x = h.reshape(-1, D_MODEL)
    t = x.shape[0]
    rows = jnp.arange(t)[:, None]
    scores = jax.nn.sigmoid(jnp.dot(x.astype(jnp.float32), w_router.astype(jnp.float32)))
    biased = scores + router_bias.astype(jnp.float32)
    grp = biased.reshape(t, N_EXPERT_GROUPS, N_EXPERTS // N_EXPERT_GROUPS)
    grp_score = jnp.sum(lax.top_k(grp, 2)[0], axis=-1)
    _, grp_idx = lax.top_k(grp_score, TOPK_GROUPS)
    grp_mask = jnp.zeros((t, N_EXPERT_GROUPS), bool).at[rows, grp_idx].set(True)
    expert_mask = jnp.repeat(grp_mask, N_EXPERTS // N_EXPERT_GROUPS, axis=1)
    _, idx = lax.top_k(jnp.where(expert_mask, biased, -jnp.inf), TOP_K)
    sel = jnp.take_along_axis(scores, idx, axis=-1)
    wts = sel / jnp.sum(sel, axis=-1, keepdims=True) * ROUTED_SCALE
    gates = jnp.zeros((t, N_EXPERTS), jnp.float32).at[rows, idx].set(wts).astype(x.dtype)

    def expert_step(acc, ew):
        wg, wu, wd, g = ew
        hid = jax.nn.silu(x @ wg) * (x @ wu)
        return acc + (g[:, None] * (hid @ wd)).astype(acc.dtype), None

    routed, _ = lax.scan(expert_step, jnp.zeros_like(x), (w_gate, w_up, w_down, gates.T))
    shared = (jax.nn.silu(x @ ws_gate) * (x @ ws_up)) @ ws_down
    return (routed + shared).reshape(*lead, D_MODEL)


def decoder_layer(x, c, pos, buf_a, buf_b, buf_c, w_ada, b_ada, norm_mix_g, w_in, conv_a_w, conv_b_w,
                  conv_b_bias, ln_b_g, ln_b_b, pool_w, pool_scale, w_out, norm_ffn_g, w_router,
                  router_bias, w_gate, w_up, w_down, ws_gate, ws_up, ws_down):
    mod = (jax.nn.silu(c) @ w_ada + b_ada).astype(x.dtype)
    sh_m, sc_m, g_m, sh_f, sc_f, g_f = jnp.split(mod[:, None, :], N_MOD, axis=-1)
    h = rms_norm(x, norm_mix_g) * (1 + sc_m) + sh_m
    proj = h @ w_in
    xa, ca, ba, a_b, g_b, p_c = jnp.split(proj, IN_SPLITS, axis=-1)
    ya, na = short_conv_mixer(xa, ca, ba, buf_a, conv_a_w)
    yb, nb = conformer_conv_mixer(a_b, g_b, buf_b, conv_b_w, conv_b_bias, ln_b_g, ln_b_b)
    yc, nc = multiscale_pool_mixer(p_c, buf_c, pos, pool_w, pool_scale)
    x = x + g_m * (jnp.concatenate([ya, yb, yc], axis=-1) @ w_out)
    h = rms_norm(x, norm_ffn_g) * (1 + sc_f) + sh_f
    x = x + g_f * moe_ffn(h, w_router, router_bias, w_gate, w_up, w_down, ws_gate, ws_up, ws_down)
    return x, na, nb, nc


def run_trunk(x, c, pos, buf_a, buf_b, buf_c, params, final_norm_g):
    new_a, new_b, new_c = [], [], []
    for l in range(DEPTH):
        layer_params = tuple(p[l] for p in params)
        x, na, nb, nc = decoder_layer(x, c, pos, buf_a[l], buf_b[l], buf_c[l], *layer_params)
        new_a.append(na)
        new_b.append(nb)
        new_c.append(nc)
    return rms_norm(x, final_norm_g), jnp.stack(new_a), jnp.stack(new_b), jnp.stack(new_c)


def setup_inputs(seed: int = 0) -> dict:
    key = jax.random.key(seed)
    ks = iter(jax.random.split(key, 40))

    def nrm(shape, s):
        return jax.random.normal(next(ks), shape, jnp.float32) * s

    L = DEPTH
    return {
        'x_prompt': nrm((BATCH, SEQ, D_MODEL), 1.0),
        'x_sample': nrm((DEC_BATCH, DEC_SEQ, D_MODEL), 1.0),
        'c_prompt': nrm((BATCH, D_MODEL), 1.0),
        'c_sample': nrm((DEC_BATCH, D_MODEL), 1.0),
        'state_conv_a': nrm((L, DEC_BATCH, CONV_A - 1, D_A), 1.0),
        'state_conv_b': nrm((L, DEC_BATCH, CONV_B - 1, D_B), 1.0),
        'state_pool': nrm((L, DEC_BATCH, POOL_BUF, D_C), 1.0),
        'w_ada': nrm((L, D_MODEL, N_MOD * D_MODEL), 0.5 * D_MODEL ** -0.5),
        'b_ada': nrm((L, N_MOD * D_MODEL), 0.02),
        'norm_mix_g': 1.0 + nrm((L, D_MODEL), 0.02),
        'w_in': nrm((L, D_MODEL, D_IN), D_MODEL ** -0.5),
        'conv_a_w': nrm((L, CONV_A, D_A), CONV_A ** -0.5),
        'conv_b_w': nrm((L, CONV_B, D_B), CONV_B ** -0.5),
        'conv_b_bias': nrm((L, D_B), 0.02),
        'ln_b_g': 1.0 + nrm((L, D_B), 0.02),
        'ln_b_b': nrm((L, D_B), 0.02),
        'pool_w': nrm((L, N_GROUPS_C, C_GROUP, C_GROUP), C_GROUP ** -0.5),
        'pool_scale': 1.0 + nrm((L, D_C), 0.1),
        'w_out': nrm((L, D_MIX, D_MODEL), D_MIX ** -0.5),
        'norm_ffn_g': 1.0 + nrm((L, D_MODEL), 0.02),
        'w_router': nrm((L, D_MODEL, N_EXPERTS), D_MODEL ** -0.5),
        'router_bias': nrm((L, N_EXPERTS), 0.01),
        'w_gate': nrm((L, N_EXPERTS, D_MODEL, D_EXPERT), D_MODEL ** -0.5),
        'w_up': nrm((L, N_EXPERTS, D_MODEL, D_EXPERT), D_MODEL ** -0.5),
        'w_down': nrm((L, N_EXPERTS, D_EXPERT, D_MODEL), D_EXPERT ** -0.5),
        'ws_gate': nrm((L, D_MODEL, D_SHARED), D_MODEL ** -0.5),
        'ws_up': nrm((L, D_MODEL, D_SHARED), D_MODEL ** -0.5),
        'ws_down': nrm((L, D_SHARED, D_MODEL), D_SHARED ** -0.5),
        'final_norm_g': 1.0 + nrm((D_MODEL,), 0.02),
    }


def reference(x_prompt, x_sample, c_prompt, c_sample, state_conv_a, state_conv_b, state_pool,
              w_ada, b_ada, norm_mix_g, w_in, conv_a_w, conv_b_w, conv_b_bias, ln_b_g, ln_b_b,
              pool_w, pool_scale, w_out, norm_ffn_g, w_router, router_bias,
              w_gate, w_up, w_down, ws_gate, ws_up, ws_down, final_norm_g):
    params = (w_ada, b_ada, norm_mix_g, w_in, conv_a_w, conv_b_w, conv_b_bias, ln_b_g, ln_b_b,
              pool_w, pool_scale, w_out, norm_ffn_g, w_router, router_bias,
              w_gate, w_up, w_down, ws_gate, ws_up, ws_down)
    n_p = x_prompt.shape[0]
    pos_prompt = jnp.arange(x_prompt.shape[1], dtype=jnp.int32)
    pos_sample = PAST_LEN + jnp.arange(x_sample.shape[1], dtype=jnp.int32)
    zero_a = jnp.zeros((DEPTH, n_p, CONV_A - 1, D_A), x_prompt.dtype)
    zero_b = jnp.zeros((DEPTH, n_p, CONV_B - 1, D_B), x_prompt.dtype)
    zero_c = jnp.zeros((DEPTH, n_p, POOL_BUF, D_C), x_prompt.dtype)
    y_prompt, na_p, nb_p, nc_p = run_trunk(x_prompt, c_prompt, pos_prompt, zero_a, zero_b, zero_c,
                                           params, final_norm_g)
    y_sample, na_s, nb_s, nc_s = run_trunk(x_sample, c_sample, pos_sample, state_conv_a, state_conv_b,
                                           state_pool, params, final_norm_g)
    return (y_prompt, y_sample, na_p, nb_p, nc_p, na_s, nb_s, nc_s)
```

```python
import functools

import jax
import jax.numpy as jnp
from jax import lax
from jax.experimental import pallas as pl
from jax.experimental.pallas import tpu as pltpu

D_MODEL = 1024
HEAD_DIM = 64
D_A = 6 * HEAD_DIM
D_B = 6 * HEAD_DIM
D_C = D_MODEL - D_A - D_B
C_GROUP = 64
POOL_WINDOWS = (2, 4, 8, 16)
POOL_BUF = max(POOL_WINDOWS) - 1
CONV_A = 3
CONV_B = 31
D_IN = 3 * D_A + 2 * D_B + D_C
N_EXPERTS = 64
N_EXPERT_GROUPS = 8
GROUP_SIZE = N_EXPERTS // N_EXPERT_GROUPS
TOPK_GROUPS = 4
TOP_K = 8
D_EXPERT = 256
ROUTED_SCALE = 2.5
N_MOD = 6
EPS = 1e-6
PAST_LEN = 16384

F32 = jnp.float32
BF16 = jnp.bfloat16

VMEM_LIMIT_BYTES = 56 * 1024 * 1024

MIX_TILE = 512
HALO_A = 8
HALO_B = 32
HALO_C = 16

MOE_BLOCK = 256
MOE_CAP = 64
MOE_SLOTS = 4
MOE_SUPER = 2048


def _silu(v):
    return v * jax.nn.sigmoid(v)


def _rms_mod(x, g, scale, shift):
    ms = jnp.mean(x * x, axis=-1, keepdims=True)
    y = (x * lax.rsqrt(ms + EPS)) * g
    return y * (1.0 + scale) + shift


def _split(v):
    hi = v.astype(BF16)
    return hi, (v - hi.astype(F32)).astype(BF16)


def _dot(a, w_hi, w_lo=None):
    a_hi, a_lo = _split(a)
    out = jnp.dot(a_hi, w_hi, preferred_element_type=F32)
    if w_lo is not None:
        out = out + jnp.dot(a_lo, w_hi, preferred_element_type=F32)
        out = out + jnp.dot(a_hi, w_lo, preferred_element_type=F32)
    return out


def _dot_ref(a, w_ref):
    return _dot(a, w_ref[0], w_ref[1] if w_ref.shape[0] == 2 else None)


def _mod_parts(mod):
    return [mod[..., i * D_MODEL:(i + 1) * D_MODEL] for i in range(N_MOD)]


def _ada_kernel(c_ref, w_ref, b_ref, o_ref):
    w_hi, w_lo = _split(w_ref[0])
    o_ref[0] = _dot(_silu(c_ref[...]), w_hi, w_lo) + b_ref[0]


def _ada(c_all, w_ada, b_ada):
    depth, _, n_out = w_ada.shape
    rows = c_all.shape[0]
    tn = 1536
    return pl.pallas_call(
        _ada_kernel,
        grid=(depth, n_out // tn),
        in_specs=[
            pl.BlockSpec((rows, D_MODEL), lambda l, n: (0, 0)),
            pl.BlockSpec((1, D_MODEL, tn), lambda l, n: (l, 0, n)),
            pl.BlockSpec((1, 1, tn), lambda l, n: (l, 0, n)),
        ],
        out_specs=pl.BlockSpec((1, rows, tn), lambda l, n: (l, 0, n)),
        out_shape=jax.ShapeDtypeStruct((depth, rows, n_out), F32),
        compiler_params=pltpu.CompilerParams(
            dimension_semantics=("arbitrary", "arbitrary"), vmem_limit_bytes=VMEM_LIMIT_BYTES),
        name="ada",
    )(c_all, w_ada, b_ada.reshape(depth, 1, n_out))


def _layer_norm(v, g, b):
    mu = jnp.mean(v, axis=-1, keepdims=True)
    d = v - mu
    var = jnp.mean(d * d, axis=-1, keepdims=True)
    return d * lax.rsqrt(var + EPS) * g + b


def _pool_select(sums, counts, p):
    lane = lax.broadcasted_iota(jnp.int32, p.shape, p.ndim - 1)
    pooled = sums[-1] / counts[-1]
    for gi in range(len(POOL_WINDOWS) - 2, -1, -1):
        pooled = jnp.where(lane < (gi + 1) * C_GROUP, sums[gi] / counts[gi], pooled)
    return pooled - p


def _router_logits_t(h2, h2_hi, wr_hi, wr_lo):
    h2_lo = (h2 - h2_hi.astype(F32)).astype(BF16)
    nt = (((1,), (1,)), ((), ()))
    lt = lax.dot_general(wr_hi, h2_hi, nt, preferred_element_type=F32)
    lt = lt + lax.dot_general(wr_hi, h2_lo, nt, preferred_element_type=F32)
    return lt + lax.dot_general(wr_lo, h2_hi, nt, preferred_element_type=F32)


def _mixer_prompt_kernel(has_prev, tile, *refs):
    if has_prev:
        x_ref, f_ref, modp_ref, *refs = refs
    else:
        x_ref, *refs = refs
    (mod_ref, ng_ref, win_ref, caw_ref, cbw_ref, cbb_ref, lng_ref, lnb_ref, pw_ref, ps_ref,
     wout_ref, nfg_ref, wrh_ref, wrl_ref,
     xmid_ref, h2_ref, lt_ref, na_ref, nb_ref, nc_ref,
     va_ref, ub_ref, pc_ref) = refs

    s = pl.program_id(1)

    @pl.when(s == 0)
    def _():
        va_ref[0:HALO_A, :] = jnp.zeros((HALO_A, D_A), F32)
        ub_ref[0:HALO_B, :] = jnp.zeros((HALO_B, D_B), F32)
        pc_ref[0:HALO_C, :] = jnp.zeros((HALO_C, D_C), F32)

    x = x_ref[0]
    if has_prev:
        x = x + _mod_parts(modp_ref[0])[5] * f_ref[0]
    sh_m, sc_m, g_m, sh_f, sc_f, g_f = _mod_parts(mod_ref[0])

    proj = _dot_ref(_rms_mod(x, ng_ref[...], sc_m, sh_m), win_ref)
    xa = proj[:, 0:D_A]
    ca = proj[:, D_A:2 * D_A]
    ba = proj[:, 2 * D_A:3 * D_A]
    a_b = proj[:, 3 * D_A:3 * D_A + D_B]
    g_b = proj[:, 3 * D_A + D_B:3 * D_A + 2 * D_B]
    p_c = proj[:, 3 * D_A + 2 * D_B:]

    va_ref[HALO_A:HALO_A + tile, :] = ca * xa
    acc = caw_ref[0:1, :] * va_ref[HALO_A - 2:HALO_A - 2 + tile, :]
    for k in range(1, CONV_A):
        acc = acc + caw_ref[k:k + 1, :] * va_ref[HALO_A - 2 + k:HALO_A - 2 + k + tile, :]
    ya = ba * acc
    na_ref[0] = va_ref[HALO_A + tile - (CONV_A - 1):HALO_A + tile, :]

    ub_ref[HALO_B:HALO_B + tile, :] = a_b * jax.nn.sigmoid(g_b)
    off = HALO_B - (CONV_B - 1)
    acc = cbw_ref[0:1, :] * ub_ref[off:off + tile, :]
    for k in range(1, CONV_B):
        acc = acc + cbw_ref[k:k + 1, :] * ub_ref[off + k:off + k + tile, :]
    yb = _silu(_layer_norm(acc + cbb_ref[...], lng_ref[...], lnb_ref[...]))
    nb_ref[0] = ub_ref[HALO_B + tile - (CONV_B - 1):HALO_B + tile, :]

    pc_ref[HALO_C:HALO_C + tile, :] = p_c
    pos = s * tile + lax.broadcasted_iota(jnp.int32, (tile, D_C), 0)
    run = p_c
    sums, counts = [], []
    for j in range(1, POOL_WINDOWS[-1]):
        run = run + pc_ref[HALO_C - j:HALO_C - j + tile, :]
        if j + 1 in POOL_WINDOWS:
            sums.append(run)
            counts.append(jnp.minimum(pos + 1, j + 1).astype(F32))
    d = _pool_select(sums, counts, p_c)
    yc = _dot_ref(d, pw_ref) * ps_ref[...]
    nc_ref[0] = pc_ref[HALO_C + tile - POOL_BUF:HALO_C + tile, :]

    va_ref[0:HALO_A, :] = va_ref[tile:tile + HALO_A, :]
    ub_ref[0:HALO_B, :] = ub_ref[tile:tile + HALO_B, :]
    pc_ref[0:HALO_C, :] = pc_ref[tile:tile + HALO_C, :]

    mix = jnp.concatenate([ya, yb, yc], axis=-1)
    xmid = x + g_m * _dot_ref(mix, wout_ref)
    xmid_ref[0] = xmid

    h2 = _rms_mod(xmid, nfg_ref[...], sc_f, sh_f)
    h2_hi = h2.astype(BF16)
    h2_ref[0] = h2_hi
    lt_ref[...] = _router_logits_t(h2, h2_hi, wrh_ref[...], wrl_ref[...])


def _full(shape):
    return pl.BlockSpec(shape, lambda *_: (0,) * len(shape))


def _mixer_prompt(x, prev, mod, lw):
    nb, seq, _ = x.shape
    tile = min(MIX_TILE, seq)
    nt = seq // tile
    tok = pl.BlockSpec((1, tile, D_MODEL), lambda b, s: (b, s, 0))
    modspec = pl.BlockSpec((1, 1, N_MOD * D_MODEL), lambda b, s: (b, 0, 0))
    args, specs = [x], [tok]
    if prev is not None:
        args += [prev[0], prev[1]]
        specs += [tok, modspec]
    args += [mod, lw["norm_mix_g"], lw["w_in"], lw["conv_a_w"], lw["conv_b_w"], lw["conv_b_bias"],
             lw["ln_b_g"], lw["ln_b_b"], lw["pool_w"], lw["pool_scale"], lw["w_out"],
             lw["norm_ffn_g"], lw["wr_hi"], lw["wr_lo"]]
    specs += [modspec] + [_full(a.shape) for a in args[len(specs) + 1:]]
    out_shape = (
        jax.ShapeDtypeStruct((nb, seq, D_MODEL), F32),
        jax.ShapeDtypeStruct((nb, seq, D_MODEL), BF16),
        jax.ShapeDtypeStruct((N_EXPERTS, nb * seq), F32),
        jax.ShapeDtypeStruct((nb, CONV_A - 1, D_A), F32),
        jax.ShapeDtypeStruct((nb, CONV_B - 1, D_B), F32),
        jax.ShapeDtypeStruct((nb, POOL_BUF, D_C), F32),
    )
    out_specs = (
        tok, tok,
        pl.BlockSpec((N_EXPERTS, tile), lambda b, s: (0, b * nt + s)),
        pl.BlockSpec((1, CONV_A - 1, D_A), lambda b, s: (b, 0, 0)),
        pl.BlockSpec((1, CONV_B - 1, D_B), lambda b, s: (b, 0, 0)),
        pl.BlockSpec((1, POOL_BUF, D_C), lambda b, s: (b, 0, 0)),
    )
    return pl.pallas_call(
        functools.partial(_mixer_prompt_kernel, prev is not None, tile),
        grid=(nb, nt),
        in_specs=specs,
        out_specs=out_specs,
        out_shape=out_shape,
        scratch_shapes=[
            pltpu.VMEM((HALO_A + tile, D_A), F32),
            pltpu.VMEM((HALO_B + tile, D_B), F32),
            pltpu.VMEM((HALO_C + tile, D_C), F32),
        ],
        compiler_params=pltpu.CompilerParams(
            dimension_semantics=("arbitrary", "arbitrary"), vmem_limit_bytes=VMEM_LIMIT_BYTES),
        name="mixer_prompt",
    )(*args)


def _mixer_sample_kernel(has_prev, *refs):
    if has_prev:
        x_ref, f_ref, modp_ref, *refs = refs
    else:
        x_ref, *refs = refs
    (mod_ref, sa_ref, sb_ref, sc_ref, ng_ref, win_ref, caw_ref, cbw_ref, cbb_ref, lng_ref, lnb_ref,
     pw_ref, ps_ref, wout_ref, nfg_ref, wrh_ref, wrl_ref,
     xmid_ref, h2_ref, lt_ref, na_ref, nb_ref, nc_ref) = refs

    t_new, n_seq, _ = x_ref.shape
    rows = t_new * n_seq
    x = x_ref[...]
    if has_prev:
        x = x + _mod_parts(modp_ref[...])[5][None] * f_ref[...]
    sh_m, sc_m, g_m, sh_f, sc_f, g_f = [m[None] for m in _mod_parts(mod_ref[...])]

    h = _rms_mod(x, ng_ref[...], sc_m, sh_m)
    proj = _dot_ref(h.reshape(rows, D_MODEL), win_ref).reshape(t_new, n_seq, D_IN)
    xa = proj[:, :, 0:D_A]
    ca = proj[:, :, D_A:2 * D_A]
    ba = proj[:, :, 2 * D_A:3 * D_A]
    a_b = proj[:, :, 3 * D_A:3 * D_A + D_B]
    g_b = proj[:, :, 3 * D_A + D_B:3 * D_A + 2 * D_B]
    p_c = proj[:, :, 3 * D_A + 2 * D_B:]

    def history(state_ref, new):
        return [state_ref[j] for j in range(state_ref.shape[0])] + [new[t] for t in range(t_new)]

    def conv(seq, w_ref, taps):
        outs = []
        for t in range(t_new):
            acc = w_ref[0:1, :] * seq[t]
            for k in range(1, taps):
                acc = acc + w_ref[k:k + 1, :] * seq[t + k]
            outs.append(acc)
        return jnp.stack(outs)

    def emit_state(out_ref, seq):
        keep = out_ref.shape[0]
        for j in range(keep):
            out_ref[j] = seq[len(seq) - keep + j]

    seq_a = history(sa_ref, ca * xa)
    ya = ba * conv(seq_a, caw_ref, CONV_A)
    emit_state(na_ref, seq_a)

    seq_b = history(sb_ref, a_b * jax.nn.sigmoid(g_b))
    vb = conv(seq_b, cbw_ref, CONV_B) + cbb_ref[...]
    yb = _silu(_layer_norm(vb, lng_ref[...], lnb_ref[...]))
    emit_state(nb_ref, seq_b)

    seq_c = history(sc_ref, p_c)
    pooled = []
    for t in range(t_new):
        cur = POOL_BUF + t
        run = seq_c[cur]
        sums, counts = [], []
        for j in range(1, POOL_WINDOWS[-1]):
            run = run + seq_c[cur - j]
            if j + 1 in POOL_WINDOWS:
                sums.append(run)
                counts.append(float(min(PAST_LEN + t + 1, j + 1)))
        pooled.append(_pool_select(sums, counts, seq_c[cur]))
    d = jnp.stack(pooled)
    yc = _dot_ref(d.reshape(rows, D_C), pw_ref).reshape(t_new, n_seq, D_C) * ps_ref[...]
    emit_state(nc_ref, seq_c)

    mix = jnp.concatenate([ya, yb, yc], axis=-1).reshape(rows, D_MODEL)
    xmid = x + g_m * _dot_ref(mix, wout_ref).reshape(t_new, n_seq, D_MODEL)
    xmid_ref[...] = xmid

    h2 = _rms_mod(xmid, nfg_ref[...], sc_f, sh_f).reshape(rows, D_MODEL)
    h2_hi = h2.astype(BF16)
    h2_ref[...] = h2_hi
    lt_ref[...] = _router_logits_t(h2, h2_hi, wrh_ref[...], wrl_ref[...])


def _mixer_sample(x, prev, mod, states, lw):
    t_new, n_seq, _ = x.shape
    rows = t_new * n_seq
    args = [x]
    if prev is not None:
        args += [prev[0], prev[1]]
    args += [mod, *states, lw["norm_mix_g"], lw["w_in"], lw["conv_a_w"], lw["conv_b_w"],
             lw["conv_b_bias"], lw["ln_b_g"], lw["ln_b_b"], lw["pool_w"], lw["pool_scale"],
             lw["w_out"], lw["norm_ffn_g"], lw["wr_hi"], lw["wr_lo"]]
    out_shape = (
        jax.ShapeDtypeStruct((t_new, n_seq, D_MODEL), F32),
        jax.ShapeDtypeStruct((rows, D_MODEL), BF16),
        jax.ShapeDtypeStruct((N_EXPERTS, rows), F32),
        jax.ShapeDtypeStruct((CONV_A - 1, n_seq, D_A), F32),
        jax.ShapeDtypeStruct((CONV_B - 1, n_seq, D_B), F32),
        jax.ShapeDtypeStruct((POOL_BUF, n_seq, D_C), F32),
    )
    return pl.pallas_call(
        functools.partial(_mixer_sample_kernel, prev is not None),
        grid=(1,),
        in_specs=[_full(a.shape) for a in args],
        out_specs=tuple(_full(o.shape) for o in out_shape),
        out_shape=out_shape,
        compiler_params=pltpu.CompilerParams(
            dimension_semantics=("arbitrary",), vmem_limit_bytes=VMEM_LIMIT_BYTES),
        name="mixer_sample",
    )(*args)


def _route(lt, bias):
    n_tok = lt.shape[1]
    scores = jax.nn.sigmoid(lt)
    biased = scores + bias
    sj = [scores[GROUP_SIZE * j:GROUP_SIZE * (j + 1)] for j in range(GROUP_SIZE)]
    bj = [biased[GROUP_SIZE * j:GROUP_SIZE * (j + 1)] for j in range(GROUP_SIZE)]

    m1 = bj[0]
    m2 = jnp.full_like(m1, -jnp.inf)
    for j in range(1, GROUP_SIZE):
        m2 = jnp.maximum(m2, jnp.minimum(m1, bj[j]))
        m1 = jnp.maximum(m1, bj[j])
    gscore = m1 + m2

    gidx = lax.broadcasted_iota(jnp.int32, (N_EXPERT_GROUPS, n_tok), 0)
    lower = [None] + [jnp.where(gidx >= k, gidx - k, gidx - k + N_EXPERT_GROUPS) < gidx
                      for k in range(1, N_EXPERT_GROUPS)]

    def beats(other, mine, tie_wins):
        return (other > mine) | ((other == mine) & tie_wins)

    grank = jnp.zeros((N_EXPERT_GROUPS, n_tok), jnp.int32)
    for k in range(1, N_EXPERT_GROUPS):
        grank = grank + beats(pltpu.roll(gscore, k, 0), gscore, lower[k]).astype(jnp.int32)
    gsel = grank < TOPK_GROUPS

    mj = [jnp.where(gsel, b, -jnp.inf) for b in bj]
    rolled = [[m] + [pltpu.roll(m, k, 0) for k in range(1, N_EXPERT_GROUPS)] for m in mj]
    sel = []
    for j in range(GROUP_SIZE):
        rank = jnp.zeros((N_EXPERT_GROUPS, n_tok), jnp.int32)
        for j2 in range(GROUP_SIZE):
            for k in range(N_EXPERT_GROUPS):
                if k == 0 and j2 == j:
                    continue
                other = rolled[j2][k]
                if k == 0:
                    won = (other >= mj[j]) if j2 < j else (other > mj[j])
                else:
                    won = beats(other, mj[j], lower[k])
                rank = rank + won.astype(jnp.int32)
        sel.append(rank < TOP_K)

    picked = [jnp.where(sel[j], sj[j], 0.0) for j in range(GROUP_SIZE)]
    sel = [jnp.where(s, 1.0, 0.0) for s in sel]
    tot = picked[0]
    for j in range(1, GROUP_SIZE):
        tot = tot + picked[j]
    denom = jnp.sum(tot, axis=0, keepdims=True)
    gates = [picked[j] / denom * ROUTED_SCALE for j in range(GROUP_SIZE)]
    return jnp.concatenate(sel, axis=0), jnp.concatenate(gates, axis=0)


def _moe_kernel(n_blocks, h2_ref, lt_ref, rb_ref, wg_ref, wu_ref, wd_ref, wsg_ref, wsu_ref,
                wsd_ref, o_ref, rank_ref, gate_ref, p_ref, xs_ref, ys_ref, gs_ref, npass_ref):
    q = pl.program_id(1)
    seg = MOE_CAP
    blk = MOE_BLOCK

    @pl.when(q == 0)
    def _():
        sel, gates = _route(lt_ref[...], rb_ref[...])
        gate_ref[...] = gates
        before = jnp.where(lax.broadcasted_iota(jnp.int32, (blk, blk), 0)
                           < lax.broadcasted_iota(jnp.int32, (blk, blk), 1), 1.0, 0.0).astype(BF16)
        most = jnp.zeros((N_EXPERTS, blk), F32)
        for b in range(n_blocks):
            sb = sel[:, b * blk:(b + 1) * blk]
            cnt = jnp.dot(sb.astype(BF16), before, preferred_element_type=F32)
            rank_ref[:, b * blk:(b + 1) * blk] = jnp.where(sb > 0.5, cnt, -1.0)
            most = jnp.maximum(most, (cnt + 1.0) * sb)
        npass_ref[0] = (jnp.max(most).astype(jnp.int32) + (seg - 1)) // seg
        x = h2_ref[...]
        hid = _silu(jnp.dot(x, wsg_ref[...], preferred_element_type=F32)) * jnp.dot(
            x, wsu_ref[...], preferred_element_type=F32)
        o_ref[...] = jnp.dot(hid.astype(BF16), wsd_ref[...], preferred_element_type=F32)

    r0 = q * MOE_SLOTS
    pos = lax.broadcasted_iota(jnp.int32, (seg, blk), 0).astype(F32)

    def one_pass(p, carry):
        base = (p * seg).astype(F32)
        for b in range(n_blocks):
            rows = []
            for s in range(MOE_SLOTS):
                rrow = rank_ref[pl.ds(r0 + s, 1), b * blk:(b + 1) * blk]
                grow = gate_ref[pl.ds(r0 + s, 1), b * blk:(b + 1) * blk]
                match = (rrow - base) == pos
                rows.append(jnp.where(match, 1.0, 0.0).astype(BF16))
                gs_ref[s, b * seg:(b + 1) * seg, :] = jnp.sum(
                    jnp.where(match, grow, 0.0), axis=1, keepdims=True)
            onehot = jnp.concatenate(rows, axis=0)
            p_ref[b] = onehot
            xs = jnp.dot(onehot, h2_ref[b * blk:(b + 1) * blk, :],
                         preferred_element_type=F32).astype(BF16)
            for s in range(MOE_SLOTS):
                xs_ref[s, b * seg:(b + 1) * seg, :] = xs[s * seg:(s + 1) * seg]
        for s in range(MOE_SLOTS):
            lhs = xs_ref[s]
            hid = _silu(jnp.dot(lhs, wg_ref[s, 0], preferred_element_type=F32)) * jnp.dot(
                lhs, wu_ref[s, 0], preferred_element_type=F32)
            y = jnp.dot(hid.astype(BF16), wd_ref[s, 0], preferred_element_type=F32)
            ys_ref[s] = (gs_ref[s] * y).astype(BF16)
        for b in range(n_blocks):
            yb = jnp.concatenate(
                [ys_ref[s, b * seg:(b + 1) * seg, :] for s in range(MOE_SLOTS)], axis=0)
            o_ref[b * blk:(b + 1) * blk, :] += lax.dot_general(
                p_ref[b], yb, (((0,), (0,)), ((), ())), preferred_element_type=F32)
        return carry

    lax.fori_loop(0, npass_ref[0], one_pass, 0)


def _moe(h2, lt, lw):
    n_tok = h2.shape[0]
    sup = min(MOE_SUPER, n_tok)
    n_blocks = sup // MOE_BLOCK
    n_steps = N_EXPERTS // MOE_SLOTS
    half = N_EXPERT_GROUPS // MOE_SLOTS
    wspec = lambda a, b: pl.BlockSpec((MOE_SLOTS, 1, a, b), lambda t, q: (q % half, q // half, 0, 0))
    return pl.pallas_call(
        functools.partial(_moe_kernel, n_blocks),
        grid=(n_tok // sup, n_steps),
        in_specs=[
            pl.BlockSpec((sup, D_MODEL), lambda t, q: (t, 0)),
            pl.BlockSpec((N_EXPERTS, sup), lambda t, q: (0, t)),
            _full((N_EXPERTS, 1)),
            wspec(D_MODEL, D_EXPERT), wspec(D_MODEL, D_EXPERT), wspec(D_EXPERT, D_MODEL),
            _full(lw["ws_gate"].shape), _full(lw["ws_up"].shape), _full(lw["ws_down"].shape),
        ],
        out_specs=pl.BlockSpec((sup, D_MODEL), lambda t, q: (t, 0)),
        out_shape=jax.ShapeDtypeStruct((n_tok, D_MODEL), F32),
        scratch_shapes=[
            pltpu.VMEM((N_EXPERTS, sup), F32),
            pltpu.VMEM((N_EXPERTS, sup), F32),
            pltpu.VMEM((n_blocks, MOE_SLOTS * MOE_CAP, MOE_BLOCK), BF16),
            pltpu.VMEM((MOE_SLOTS, n_blocks * MOE_CAP, D_MODEL), BF16),
            pltpu.VMEM((MOE_SLOTS, n_blocks * MOE_CAP, D_MODEL), BF16),
            pltpu.VMEM((MOE_SLOTS, n_blocks * MOE_CAP, 1), F32),
            pltpu.SMEM((1,), jnp.int32),
        ],
        compiler_params=pltpu.CompilerParams(
            dimension_semantics=("arbitrary", "arbitrary"), vmem_limit_bytes=VMEM_LIMIT_BYTES),
        name="moe",
    )(h2, lt, lw["router_bias"], lw["w_gate"], lw["w_up"], lw["w_down"],
      lw["ws_gate"], lw["ws_up"], lw["ws_down"])


def _final_kernel(x_ref, f_ref, mod_ref, g_ref, o_ref):
    gate = _mod_parts(mod_ref[...])[5]
    if x_ref.ndim == 3 and mod_ref.ndim == 2:
        gate = gate[None]
    x = x_ref[...] + gate * f_ref[...]
    ms = jnp.mean(x * x, axis=-1, keepdims=True)
    o_ref[...] = (x * lax.rsqrt(ms + EPS)) * g_ref[...]


def _final_prompt(xmid, ffn, mod, g):
    nb, seq, _ = xmid.shape
    tile = min(MIX_TILE, seq)
    tok = pl.BlockSpec((1, tile, D_MODEL), lambda b, s: (b, s, 0))
    return pl.pallas_call(
        _final_kernel,
        grid=(nb, seq // tile),
        in_specs=[tok, tok, pl.BlockSpec((1, 1, N_MOD * D_MODEL), lambda b, s: (b, 0, 0)),
                  _full(g.shape)],
        out_specs=tok,
        out_shape=jax.ShapeDtypeStruct(xmid.shape, F32),
        compiler_params=pltpu.CompilerParams(
            dimension_semantics=("arbitrary", "arbitrary"), vmem_limit_bytes=VMEM_LIMIT_BYTES),
        name="final_prompt",
    )(xmid, ffn, mod, g)


def _final_sample(xmid, ffn, mod, g):
    return pl.pallas_call(
        _final_kernel,
        grid=(1,),
        in_specs=[_full(xmid.shape), _full(ffn.shape), _full(mod.shape), _full(g.shape)],
        out_specs=_full(xmid.shape),
        out_shape=jax.ShapeDtypeStruct(xmid.shape, F32),
        compiler_params=pltpu.CompilerParams(
            dimension_semantics=("arbitrary",), vmem_limit_bytes=VMEM_LIMIT_BYTES),
        name="final_sample",
    )(xmid, ffn, mod, g)


def _slot_order(a, axis):
    shape = a.shape
    a = a.reshape(shape[:axis] + (N_EXPERT_GROUPS, GROUP_SIZE) + shape[axis + 1:])
    return jnp.swapaxes(a, axis, axis + 1).reshape(shape)


def _layer_weights(l, w_in, norm_mix_g, conv_a_w, conv_b_w, conv_b_bias, ln_b_g, ln_b_b, pool_w,
                   pool_scale, w_out, norm_ffn_g, w_router, router_bias, w_gate, w_up, w_down,
                   ws_gate, ws_up, ws_down):
    row = lambda v: v[l].reshape(1, -1)
    wr = _slot_order(w_router[l], 1).T
    wr_hi = wr.astype(BF16)
    blockdiag = jax.scipy.linalg.block_diag(*[pool_w[l, g] for g in range(pool_w.shape[1])])
    experts = lambda w: w[l].astype(BF16).reshape(
        (N_EXPERT_GROUPS, GROUP_SIZE) + w.shape[2:])
    precise = l + 1 < w_in.shape[0]
    stacked = lambda w: jnp.stack(_split(w)) if precise else w.astype(BF16)[None]
    return dict(
        norm_mix_g=row(norm_mix_g), w_in=stacked(w_in[l]), conv_a_w=conv_a_w[l],
        conv_b_w=conv_b_w[l], conv_b_bias=row(conv_b_bias), ln_b_g=row(ln_b_g), ln_b_b=row(ln_b_b),
        pool_w=stacked(blockdiag), pool_scale=row(pool_scale), w_out=stacked(w_out[l]),
        norm_ffn_g=row(norm_ffn_g), wr_hi=wr_hi, wr_lo=(wr - wr_hi.astype(F32)).astype(BF16),
        router_bias=_slot_order(router_bias[l], 0).reshape(N_EXPERTS, 1),
        w_gate=experts(w_gate), w_up=experts(w_up), w_down=experts(w_down),
        ws_gate=ws_gate[l].astype(BF16), ws_up=ws_up[l].astype(BF16), ws_down=ws_down[l].astype(BF16),
    )


def kernel(x_prompt, x_sample, c_prompt, c_sample, state_conv_a, state_conv_b, state_pool, w_ada, b_ada, norm_mix_g, w_in, conv_a_w, conv_b_w, conv_b_bias, ln_b_g, ln_b_b, pool_w, pool_scale, w_out, norm_ffn_g, w_router, router_bias, w_gate, w_up, w_down, ws_gate, ws_up, ws_down, final_norm_g):
    depth = w_ada.shape[0]
    n_p, seq, _ = x_prompt.shape
    n_s, t_new, _ = x_sample.shape

    mod = _ada(jnp.concatenate([c_prompt, c_sample], axis=0), w_ada, b_ada)
    mod_p = mod[:, :n_p].reshape(depth, n_p, 1, N_MOD * D_MODEL)
    mod_s = mod[:, n_p:]

    xp = x_prompt
    xs = jnp.swapaxes(x_sample, 0, 1)
    time_major = lambda st: jnp.swapaxes(st, 1, 2)
    st_a, st_b, st_c = time_major(state_conv_a), time_major(state_conv_b), time_major(state_pool)

    prev_p = prev_s = None
    new_p, new_s = [], []
    for l in range(depth):
        lw = _layer_weights(l, w_in, norm_mix_g, conv_a_w, conv_b_w, conv_b_bias, ln_b_g, ln_b_b,
                            pool_w, pool_scale, w_out, norm_ffn_g, w_router, router_bias,
                            w_gate, w_up, w_down, ws_gate, ws_up, ws_down)
        xp, h2p, ltp, na, nb, nc = _mixer_prompt(xp, prev_p, mod_p[l], lw)
        new_p.append((na, nb, nc))
        ffn_p = _moe(h2p.reshape(n_p * seq, D_MODEL), ltp, lw).reshape(n_p, seq, D_MODEL)
        prev_p = (ffn_p, mod_p[l])

        xs, h2s, lts, na, nb, nc = _mixer_sample(xs, prev_s, mod_s[l], (st_a[l], st_b[l], st_c[l]), lw)
        new_s.append((na, nb, nc))
        ffn_s = _moe(h2s, lts, lw).reshape(t_new, n_s, D_MODEL)
        prev_s = (ffn_s, mod_s[l])

    g = final_norm_g.reshape(1, D_MODEL)
    y_prompt = _final_prompt(xp, prev_p[0], prev_p[1], g)
    y_sample = jnp.swapaxes(_final_sample(xs, prev_s[0], prev_s[1], g), 0, 1)

    stack = lambda items, i: jnp.stack([it[i] for it in items])
    batch_major = lambda a: jnp.swapaxes(a, 1, 2)
    return (y_prompt, y_sample,
            stack(new_p, 0), stack(new_p, 1), stack(new_p, 2),
            batch_major(stack(new_s, 0)), batch_major(stack(new_s, 1)), batch_major(stack(new_s, 2)))
```

```python
import functools

import jax
import jax.numpy as jnp
from jax import lax
from jax.experimental import pallas as pl
from jax.experimental.pallas import tpu as pltpu

D_MODEL = 1024
HEAD_DIM = 64
D_A = 6 * HEAD_DIM
D_B = 6 * HEAD_DIM
D_C = D_MODEL - D_A - D_B
C_GROUP = 64
POOL_WINDOWS = (2, 4, 8, 16)
POOL_BUF = max(POOL_WINDOWS) - 1
CONV_A = 3
CONV_B = 31
D_IN = 3 * D_A + 2 * D_B + D_C
N_EXPERTS = 64
N_EXPERT_GROUPS = 8
GROUP_SIZE = N_EXPERTS // N_EXPERT_GROUPS
TOPK_GROUPS = 4
TOP_K = 8
D_EXPERT = 256
ROUTED_SCALE = 2.5
N_MOD = 6
EPS = 1e-6
PAST_LEN = 16384

F32 = jnp.float32
BF16 = jnp.bfloat16

VMEM_LIMIT_BYTES = 56 * 1024 * 1024

MIX_TILE = 512
HALO_A = 8
HALO_B = 32
HALO_C = 16

MOE_BLOCK = 256
MOE_CAP = 64
MOE_MAIN = 48
MOE_SLOTS = 4
MOE_SUPER = 2048


def _silu(v):
    return v * jax.nn.sigmoid(v)


def _rms_mod(x, g, scale, shift):
    ms = jnp.mean(x * x, axis=-1, keepdims=True)
    y = (x * lax.rsqrt(ms + EPS)) * g
    return y * (1.0 + scale) + shift


def _split(v):
    hi = v.astype(BF16)
    return hi, (v - hi.astype(F32)).astype(BF16)


def _dot(a, w_hi, w_lo=None):
    a_hi, a_lo = _split(a)
    out = jnp.dot(a_hi, w_hi, preferred_element_type=F32)
    if w_lo is not None:
        out = out + jnp.dot(a_lo, w_hi, preferred_element_type=F32)
        out = out + jnp.dot(a_hi, w_lo, preferred_element_type=F32)
    return out


def _dot_ref(a, w_ref):
    return _dot(a, w_ref[0], w_ref[1] if w_ref.shape[0] == 2 else None)


def _mod_parts(mod):
    return [mod[..., i * D_MODEL:(i + 1) * D_MODEL] for i in range(N_MOD)]


def _ada_kernel(c_ref, w_ref, b_ref, o_ref):
    w_hi, w_lo = _split(w_ref[0])
    o_ref[0] = _dot(_silu(c_ref[...]), w_hi, w_lo) + b_ref[0]


def _ada(c_all, w_ada, b_ada):
    depth, _, n_out = w_ada.shape
    rows = c_all.shape[0]
    tn = 1536
    return pl.pallas_call(
        _ada_kernel,
        grid=(depth, n_out // tn),
        in_specs=[
            pl.BlockSpec((rows, D_MODEL), lambda l, n: (0, 0)),
            pl.BlockSpec((1, D_MODEL, tn), lambda l, n: (l, 0, n)),
            pl.BlockSpec((1, 1, tn), lambda l, n: (l, 0, n)),
        ],
        out_specs=pl.BlockSpec((1, rows, tn), lambda l, n: (l, 0, n)),
        out_shape=jax.ShapeDtypeStruct((depth, rows, n_out), F32),
        compiler_params=pltpu.CompilerParams(
            dimension_semantics=("arbitrary", "arbitrary"), vmem_limit_bytes=VMEM_LIMIT_BYTES),
        name="ada",
    )(c_all, w_ada, b_ada.reshape(depth, 1, n_out))


def _layer_norm(v, g, b):
    mu = jnp.mean(v, axis=-1, keepdims=True)
    d = v - mu
    var = jnp.mean(d * d, axis=-1, keepdims=True)
    return d * lax.rsqrt(var + EPS) * g + b


def _pool_select(sums, counts, p):
    lane = lax.broadcasted_iota(jnp.int32, p.shape, p.ndim - 1)
    pooled = sums[-1] / counts[-1]
    for gi in range(len(POOL_WINDOWS) - 2, -1, -1):
        pooled = jnp.where(lane < (gi + 1) * C_GROUP, sums[gi] / counts[gi], pooled)
    return pooled - p


def _router_logits_t(h2, h2_hi, wr_hi, wr_lo):
    h2_lo = (h2 - h2_hi.astype(F32)).astype(BF16)
    nt = (((1,), (1,)), ((), ()))
    lt = lax.dot_general(wr_hi, h2_hi, nt, preferred_element_type=F32)
    lt = lt + lax.dot_general(wr_hi, h2_lo, nt, preferred_element_type=F32)
    return lt + lax.dot_general(wr_lo, h2_hi, nt, preferred_element_type=F32)


def _mixer_prompt_kernel(has_prev, tile, *refs):
    if has_prev:
        x_ref, f_ref, modp_ref, *refs = refs
    else:
        x_ref, *refs = refs
    (mod_ref, ng_ref, win_ref, caw_ref, cbw_ref, cbb_ref, lng_ref, lnb_ref, pw_ref, ps_ref,
     wout_ref, nfg_ref, wrh_ref, wrl_ref,
     xmid_ref, h2_ref, lt_ref, na_ref, nb_ref, nc_ref,
     va_ref, ub_ref, pc_ref) = refs

    s = pl.program_id(1)

    @pl.when(s == 0)
    def _():
        va_ref[0:HALO_A, :] = jnp.zeros((HALO_A, D_A), F32)
        ub_ref[0:HALO_B, :] = jnp.zeros((HALO_B, D_B), F32)
        pc_ref[0:HALO_C, :] = jnp.zeros((HALO_C, D_C), F32)

    x = x_ref[0]
    if has_prev:
        x = x + _mod_parts(modp_ref[0])[5] * f_ref[0]
    sh_m, sc_m, g_m, sh_f, sc_f, g_f = _mod_parts(mod_ref[0])

    proj = _dot_ref(_rms_mod(x, ng_ref[...], sc_m, sh_m), win_ref)
    xa = proj[:, 0:D_A]
    ca = proj[:, D_A:2 * D_A]
    ba = proj[:, 2 * D_A:3 * D_A]
    a_b = proj[:, 3 * D_A:3 * D_A + D_B]
    g_b = proj[:, 3 * D_A + D_B:3 * D_A + 2 * D_B]
    p_c = proj[:, 3 * D_A + 2 * D_B:]

    va_ref[HALO_A:HALO_A + tile, :] = ca * xa
    acc = caw_ref[0:1, :] * va_ref[HALO_A - 2:HALO_A - 2 + tile, :]
    for k in range(1, CONV_A):
        acc = acc + caw_ref[k:k + 1, :] * va_ref[HALO_A - 2 + k:HALO_A - 2 + k + tile, :]
    ya = ba * acc
    na_ref[0] = va_ref[HALO_A + tile - (CONV_A - 1):HALO_A + tile, :]

    ub_ref[HALO_B:HALO_B + tile, :] = a_b * jax.nn.sigmoid(g_b)
    off = HALO_B - (CONV_B - 1)
    acc = cbw_ref[0:1, :] * ub_ref[off:off + tile, :]
    for k in range(1, CONV_B):
        acc = acc + cbw_ref[k:k + 1, :] * ub_ref[off + k:off + k + tile, :]
    yb = _silu(_layer_norm(acc + cbb_ref[...], lng_ref[...], lnb_ref[...]))
    nb_ref[0] = ub_ref[HALO_B + tile - (CONV_B - 1):HALO_B + tile, :]

    pc_ref[HALO_C:HALO_C + tile, :] = p_c
    pos = s * tile + lax.broadcasted_iota(jnp.int32, (tile, D_C), 0)
    run = p_c
    sums, counts = [], []
    for j in range(1, POOL_WINDOWS[-1]):
        run = run + pc_ref[HALO_C - j:HALO_C - j + tile, :]
        if j + 1 in POOL_WINDOWS:
            sums.append(run)
            counts.append(jnp.minimum(pos + 1, j + 1).astype(F32))
    d = _pool_select(sums, counts, p_c)
    yc = _dot_ref(d, pw_ref) * ps_ref[...]
    nc_ref[0] = pc_ref[HALO_C + tile - POOL_BUF:HALO_C + tile, :]

    va_ref[0:HALO_A, :] = va_ref[tile:tile + HALO_A, :]
    ub_ref[0:HALO_B, :] = ub_ref[tile:tile + HALO_B, :]
    pc_ref[0:HALO_C, :] = pc_ref[tile:tile + HALO_C, :]

    mix = jnp.concatenate([ya, yb, yc], axis=-1)
    xmid = x + g_m * _dot_ref(mix, wout_ref)
    xmid_ref[0] = xmid

    h2 = _rms_mod(xmid, nfg_ref[...], sc_f, sh_f)
    h2_hi = h2.astype(BF16)
    h2_ref[0] = h2_hi
    lt_ref[...] = _router_logits_t(h2, h2_hi, wrh_ref[...], wrl_ref[...])


def _full(shape):
    return pl.BlockSpec(shape, lambda *_: (0,) * len(shape))


def _mixer_prompt(x, prev, mod, lw):
    nb, seq, _ = x.shape
    tile = min(MIX_TILE, seq)
    nt = seq // tile
    tok = pl.BlockSpec((1, tile, D_MODEL), lambda b, s: (b, s, 0))
    modspec = pl.BlockSpec((1, 1, N_MOD * D_MODEL), lambda b, s: (b, 0, 0))
    args, specs = [x], [tok]
    if prev is not None:
        args += [prev[0], prev[1]]
        specs += [tok, modspec]
    args += [mod, lw["norm_mix_g"], lw["w_in"], lw["conv_a_w"], lw["conv_b_w"], lw["conv_b_bias"],
             lw["ln_b_g"], lw["ln_b_b"], lw["pool_w"], lw["pool_scale"], lw["w_out"],
             lw["norm_ffn_g"], lw["wr_hi"], lw["wr_lo"]]
    specs += [modspec] + [_full(a.shape) for a in args[len(specs) + 1:]]
    out_shape = (
        jax.ShapeDtypeStruct((nb, seq, D_MODEL), F32),
        jax.ShapeDtypeStruct((nb, seq, D_MODEL), BF16),
        jax.ShapeDtypeStruct((N_EXPERTS, nb * seq), F32),
        jax.ShapeDtypeStruct((nb, CONV_A - 1, D_A), F32),
        jax.ShapeDtypeStruct((nb, CONV_B - 1, D_B), F32),
        jax.ShapeDtypeStruct((nb, POOL_BUF, D_C), F32),
    )
    out_specs = (
        tok, tok,
        pl.BlockSpec((N_EXPERTS, tile), lambda b, s: (0, b * nt + s)),
        pl.BlockSpec((1, CONV_A - 1, D_A), lambda b, s: (b, 0, 0)),
        pl.BlockSpec((1, CONV_B - 1, D_B), lambda b, s: (b, 0, 0)),
        pl.BlockSpec((1, POOL_BUF, D_C), lambda b, s: (b, 0, 0)),
    )
    return pl.pallas_call(
        functools.partial(_mixer_prompt_kernel, prev is not None, tile),
        grid=(nb, nt),
        in_specs=specs,
        out_specs=out_specs,
        out_shape=out_shape,
        scratch_shapes=[
            pltpu.VMEM((HALO_A + tile, D_A), F32),
            pltpu.VMEM((HALO_B + tile, D_B), F32),
            pltpu.VMEM((HALO_C + tile, D_C), F32),
        ],
        compiler_params=pltpu.CompilerParams(
            dimension_semantics=("arbitrary", "arbitrary"), vmem_limit_bytes=VMEM_LIMIT_BYTES),
        name="mixer_prompt",
    )(*args)


def _mixer_sample_kernel(has_prev, *refs):
    if has_prev:
        x_ref, f_ref, modp_ref, *refs = refs
    else:
        x_ref, *refs = refs
    (mod_ref, sa_ref, sb_ref, sc_ref, ng_ref, win_ref, caw_ref, cbw_ref, cbb_ref, lng_ref, lnb_ref,
     pw_ref, ps_ref, wout_ref, nfg_ref, wrh_ref, wrl_ref,
     xmid_ref, h2_ref, lt_ref, na_ref, nb_ref, nc_ref) = refs

    t_new, n_seq, _ = x_ref.shape
    rows = t_new * n_seq
    x = x_ref[...]
    if has_prev:
        x = x + _mod_parts(modp_ref[...])[5][None] * f_ref[...]
    sh_m, sc_m, g_m, sh_f, sc_f, g_f = [m[None] for m in _mod_parts(mod_ref[...])]

    h = _rms_mod(x, ng_ref[...], sc_m, sh_m)
    proj = _dot_ref(h.reshape(rows, D_MODEL), win_ref).reshape(t_new, n_seq, D_IN)
    xa = proj[:, :, 0:D_A]
    ca = proj[:, :, D_A:2 * D_A]
    ba = proj[:, :, 2 * D_A:3 * D_A]
    a_b = proj[:, :, 3 * D_A:3 * D_A + D_B]
    g_b = proj[:, :, 3 * D_A + D_B:3 * D_A + 2 * D_B]
    p_c = proj[:, :, 3 * D_A + 2 * D_B:]

    def history(state_ref, new):
        return [state_ref[j] for j in range(state_ref.shape[0])] + [new[t] for t in range(t_new)]

    def conv(seq, w_ref, taps):
        outs = []
        for t in range(t_new):
            acc = w_ref[0:1, :] * seq[t]
            for k in range(1, taps):
                acc = acc + w_ref[k:k + 1, :] * seq[t + k]
            outs.append(acc)
        return jnp.stack(outs)

    def emit_state(out_ref, seq):
        keep = out_ref.shape[0]
        for j in range(keep):
            out_ref[j] = seq[len(seq) - keep + j]

    seq_a = history(sa_ref, ca * xa)
    ya = ba * conv(seq_a, caw_ref, CONV_A)
    emit_state(na_ref, seq_a)

    seq_b = history(sb_ref, a_b * jax.nn.sigmoid(g_b))
    vb = conv(seq_b, cbw_ref, CONV_B) + cbb_ref[...]
    yb = _silu(_layer_norm(vb, lng_ref[...], lnb_ref[...]))
    emit_state(nb_ref, seq_b)

    seq_c = history(sc_ref, p_c)
    pooled = []
    for t in range(t_new):
        cur = POOL_BUF + t
        run = seq_c[cur]
        sums, counts = [], []
        for j in range(1, POOL_WINDOWS[-1]):
            run = run + seq_c[cur - j]
            if j + 1 in POOL_WINDOWS:
                sums.append(run)
                counts.append(float(min(PAST_LEN + t + 1, j + 1)))
        pooled.append(_pool_select(sums, counts, seq_c[cur]))
    d = jnp.stack(pooled)
    yc = _dot_ref(d.reshape(rows, D_C), pw_ref).reshape(t_new, n_seq, D_C) * ps_ref[...]
    emit_state(nc_ref, seq_c)

    mix = jnp.concatenate([ya, yb, yc], axis=-1).reshape(rows, D_MODEL)
    xmid = x + g_m * _dot_ref(mix, wout_ref).reshape(t_new, n_seq, D_MODEL)
    xmid_ref[...] = xmid

    h2 = _rms_mod(xmid, nfg_ref[...], sc_f, sh_f).reshape(rows, D_MODEL)
    h2_hi = h2.astype(BF16)
    h2_ref[...] = h2_hi
    lt_ref[...] = _router_logits_t(h2, h2_hi, wrh_ref[...], wrl_ref[...])


def _mixer_sample(x, prev, mod, states, lw):
    t_new, n_seq, _ = x.shape
    rows = t_new * n_seq
    args = [x]
    if prev is not None:
        args += [prev[0], prev[1]]
    args += [mod, *states, lw["norm_mix_g"], lw["w_in"], lw["conv_a_w"], lw["conv_b_w"],
             lw["conv_b_bias"], lw["ln_b_g"], lw["ln_b_b"], lw["pool_w"], lw["pool_scale"],
             lw["w_out"], lw["norm_ffn_g"], lw["wr_hi"], lw["wr_lo"]]
    out_shape = (
        jax.ShapeDtypeStruct((t_new, n_seq, D_MODEL), F32),
        jax.ShapeDtypeStruct((rows, D_MODEL), BF16),
        jax.ShapeDtypeStruct((N_EXPERTS, rows), F32),
        jax.ShapeDtypeStruct((CONV_A - 1, n_seq, D_A), F32),
        jax.ShapeDtypeStruct((CONV_B - 1, n_seq, D_B), F32),
        jax.ShapeDtypeStruct((POOL_BUF, n_seq, D_C), F32),
    )
    return pl.pallas_call(
        functools.partial(_mixer_sample_kernel, prev is not None),
        grid=(1,),
        in_specs=[_full(a.shape) for a in args],
        out_specs=tuple(_full(o.shape) for o in out_shape),
        out_shape=out_shape,
        compiler_params=pltpu.CompilerParams(
            dimension_semantics=("arbitrary",), vmem_limit_bytes=VMEM_LIMIT_BYTES),
        name="mixer_sample",
    )(*args)


def _route(lt, bias):
    n_tok = lt.shape[1]
    scores = jax.nn.sigmoid(lt)
    biased = scores + bias
    sj = [scores[GROUP_SIZE * j:GROUP_SIZE * (j + 1)] for j in range(GROUP_SIZE)]
    bj = [biased[GROUP_SIZE * j:GROUP_SIZE * (j + 1)] for j in range(GROUP_SIZE)]

    m1 = bj[0]
    m2 = jnp.full_like(m1, -jnp.inf)
    for j in range(1, GROUP_SIZE):
        m2 = jnp.maximum(m2, jnp.minimum(m1, bj[j]))
        m1 = jnp.maximum(m1, bj[j])
    gscore = m1 + m2

    gidx = lax.broadcasted_iota(jnp.int32, (N_EXPERT_GROUPS, n_tok), 0)
    lower = [None] + [jnp.where(gidx >= k, gidx - k, gidx - k + N_EXPERT_GROUPS) < gidx
                      for k in range(1, N_EXPERT_GROUPS)]

    def beats(other, mine, tie_wins):
        return (other > mine) | ((other == mine) & tie_wins)

    grank = jnp.zeros((N_EXPERT_GROUPS, n_tok), jnp.int32)
    for k in range(1, N_EXPERT_GROUPS):
        grank = grank + beats(pltpu.roll(gscore, k, 0), gscore, lower[k]).astype(jnp.int32)
    gsel = grank < TOPK_GROUPS

    mj = [jnp.where(gsel, b, -jnp.inf) for b in bj]
    rolled = [[m] + [pltpu.roll(m, k, 0) for k in range(1, N_EXPERT_GROUPS)] for m in mj]
    sel = []
    for j in range(GROUP_SIZE):
        rank = jnp.zeros((N_EXPERT_GROUPS, n_tok), jnp.int32)
        for j2 in range(GROUP_SIZE):
            for k in range(N_EXPERT_GROUPS):
                if k == 0 and j2 == j:
                    continue
                other = rolled[j2][k]
                if k == 0:
                    won = (other >= mj[j]) if j2 < j else (other > mj[j])
                else:
                    won = beats(other, mj[j], lower[k])
                rank = rank + won.astype(jnp.int32)
        sel.append(rank < TOP_K)

    picked = [jnp.where(sel[j], sj[j], 0.0) for j in range(GROUP_SIZE)]
    sel = [jnp.where(s, 1.0, 0.0) for s in sel]
    tot = picked[0]
    for j in range(1, GROUP_SIZE):
        tot = tot + picked[j]
    denom = jnp.sum(tot, axis=0, keepdims=True)
    gates = [picked[j] / denom * ROUTED_SCALE for j in range(GROUP_SIZE)]
    return jnp.concatenate(sel, axis=0), jnp.concatenate(gates, axis=0)


def _moe_kernel(n_blocks, h2_ref, lt_ref, rb_ref, wg_ref, wu_ref, wd_ref, wsg_ref, wsu_ref,
                wsd_ref, o_ref, rank_ref, gate_ref, p_ref, xs_ref, ys_ref, gs_ref, most_ref):
    q = pl.program_id(1)
    seg = MOE_CAP
    blk = MOE_BLOCK

    @pl.when(q == 0)
    def _():
        sel, gates = _route(lt_ref[...], rb_ref[...])
        gate_ref[...] = gates
        before = jnp.where(lax.broadcasted_iota(jnp.int32, (blk, blk), 0)
                           < lax.broadcasted_iota(jnp.int32, (blk, blk), 1), 1.0, 0.0).astype(BF16)
        most = jnp.zeros((N_EXPERTS, blk), F32)
        for b in range(n_blocks):
            sb = sel[:, b * blk:(b + 1) * blk]
            cnt = jnp.dot(sb.astype(BF16), before, preferred_element_type=F32)
            rank_ref[:, b * blk:(b + 1) * blk] = jnp.where(sb > 0.5, cnt, -1.0)
            most = jnp.maximum(most, (cnt + 1.0) * sb)
        for r in range(N_EXPERTS):
            most_ref[r] = jnp.max(most[r:r + 1, :]).astype(jnp.int32)
        x = h2_ref[...]
        hid = _silu(jnp.dot(x, wsg_ref[...], preferred_element_type=F32)) * jnp.dot(
            x, wsu_ref[...], preferred_element_type=F32)
        o_ref[...] = jnp.dot(hid.astype(BF16), wsd_ref[...], preferred_element_type=F32)

    @pl.when((q == 0) & (pl.program_id(0) == 0))
    def _():
        ys_ref[...] = jnp.zeros(ys_ref.shape, BF16)

    r0 = q * MOE_SLOTS
    pos = lax.broadcasted_iota(jnp.int32, (seg, blk), 0).astype(F32)
    most = [most_ref[r0 + s] for s in range(MOE_SLOTS)]
    step_most = functools.reduce(jnp.maximum, most)

    def expert_mlp(s, lo, hi):
        n_rows = n_blocks * (hi - lo)
        lhs = xs_ref[s, :, lo:hi, :].reshape(n_rows, D_MODEL)
        hid = _silu(jnp.dot(lhs, wg_ref[s, 0], preferred_element_type=F32)) * jnp.dot(
            lhs, wu_ref[s, 0], preferred_element_type=F32)
        y = jnp.dot(hid.astype(BF16), wd_ref[s, 0], preferred_element_type=F32)
        y = gs_ref[s, :, lo:hi, :].reshape(n_rows, 1) * y
        ys_ref[s, :, lo:hi, :] = y.astype(BF16).reshape(n_blocks, hi - lo, D_MODEL)

    def one_pass(p, carry):
        first = p * seg
        base = first.astype(F32)
        for b in range(n_blocks):
            rows = []
            for s in range(MOE_SLOTS):
                rrow = rank_ref[pl.ds(r0 + s, 1), b * blk:(b + 1) * blk]
                grow = gate_ref[pl.ds(r0 + s, 1), b * blk:(b + 1) * blk]
                match = (rrow - base) == pos
                rows.append(jnp.where(match, 1.0, 0.0).astype(BF16))
                gs_ref[s, b] = jnp.sum(jnp.where(match, grow, 0.0), axis=1, keepdims=True)
            onehot = jnp.concatenate(rows, axis=0)
            p_ref[b] = onehot
            xs = jnp.dot(onehot, h2_ref[b * blk:(b + 1) * blk, :],
                         preferred_element_type=F32).astype(BF16)
            for s in range(MOE_SLOTS):
                xs_ref[s, b] = xs[s * seg:(s + 1) * seg]
        for s in range(MOE_SLOTS):
            pl.when(most[s] > first)(functools.partial(expert_mlp, s, 0, MOE_MAIN))
            pl.when(most[s] > first + MOE_MAIN)(functools.partial(expert_mlp, s, MOE_MAIN, seg))
        for b in range(n_blocks):
            yb = jnp.concatenate([ys_ref[s, b] for s in range(MOE_SLOTS)], axis=0)
            o_ref[b * blk:(b + 1) * blk, :] += lax.dot_general(
                p_ref[b], yb, (((0,), (0,)), ((), ())), preferred_element_type=F32)
        return carry

    lax.fori_loop(0, (step_most + (seg - 1)) // seg, one_pass, 0)


def _moe(h2, lt, lw):
    n_tok = h2.shape[0]
    sup = min(MOE_SUPER, n_tok)
    n_blocks = sup // MOE_BLOCK
    n_steps = N_EXPERTS // MOE_SLOTS
    half = N_EXPERT_GROUPS // MOE_SLOTS
    wspec = lambda a, b: pl.BlockSpec((MOE_SLOTS, 1, a, b), lambda t, q: (q % half, q // half, 0, 0))
    return pl.pallas_call(
        functools.partial(_moe_kernel, n_blocks),
        grid=(n_tok // sup, n_steps),
        in_specs=[
            pl.BlockSpec((sup, D_MODEL), lambda t, q: (t, 0)),
            pl.BlockSpec((N_EXPERTS, sup), lambda t, q: (0, t)),
            _full((N_EXPERTS, 1)),
            wspec(D_MODEL, D_EXPERT), wspec(D_MODEL, D_EXPERT), wspec(D_EXPERT, D_MODEL),
            _full(lw["ws_gate"].shape), _full(lw["ws_up"].shape), _full(lw["ws_down"].shape),
        ],
        out_specs=pl.BlockSpec((sup, D_MODEL), lambda t, q: (t, 0)),
        out_shape=jax.ShapeDtypeStruct((n_tok, D_MODEL), F32),
        scratch_shapes=[
            pltpu.VMEM((N_EXPERTS, sup), F32),
            pltpu.VMEM((N_EXPERTS, sup), F32),
            pltpu.VMEM((n_blocks, MOE_SLOTS * MOE_CAP, MOE_BLOCK), BF16),
            pltpu.VMEM((MOE_SLOTS, n_blocks, MOE_CAP, D_MODEL), BF16),
            pltpu.VMEM((MOE_SLOTS, n_blocks, MOE_CAP, D_MODEL), BF16),
            pltpu.VMEM((MOE_SLOTS, n_blocks, MOE_CAP, 1), F32),
            pltpu.SMEM((N_EXPERTS,), jnp.int32),
        ],
        compiler_params=pltpu.CompilerParams(
            dimension_semantics=("arbitrary", "arbitrary"), vmem_limit_bytes=VMEM_LIMIT_BYTES),
        name="moe",
    )(h2, lt, lw["router_bias"], lw["w_gate"], lw["w_up"], lw["w_down"],
      lw["ws_gate"], lw["ws_up"], lw["ws_down"])


def _final_kernel(x_ref, f_ref, mod_ref, g_ref, o_ref):
    gate = _mod_parts(mod_ref[...])[5]
    if x_ref.ndim == 3 and mod_ref.ndim == 2:
        gate = gate[None]
    x = x_ref[...] + gate * f_ref[...]
    ms = jnp.mean(x * x, axis=-1, keepdims=True)
    o_ref[...] = (x * lax.rsqrt(ms + EPS)) * g_ref[...]


def _final_prompt(xmid, ffn, mod, g):
    nb, seq, _ = xmid.shape
    tile = min(MIX_TILE, seq)
    tok = pl.BlockSpec((1, tile, D_MODEL), lambda b, s: (b, s, 0))
    return pl.pallas_call(
        _final_kernel,
        grid=(nb, seq // tile),
        in_specs=[tok, tok, pl.BlockSpec((1, 1, N_MOD * D_MODEL), lambda b, s: (b, 0, 0)),
                  _full(g.shape)],
        out_specs=tok,
        out_shape=jax.ShapeDtypeStruct(xmid.shape, F32),
        compiler_params=pltpu.CompilerParams(
            dimension_semantics=("arbitrary", "arbitrary"), vmem_limit_bytes=VMEM_LIMIT_BYTES),
        name="final_prompt",
    )(xmid, ffn, mod, g)


def _final_sample(xmid, ffn, mod, g):
    return pl.pallas_call(
        _final_kernel,
        grid=(1,),
        in_specs=[_full(xmid.shape), _full(ffn.shape), _full(mod.shape), _full(g.shape)],
        out_specs=_full(xmid.shape),
        out_shape=jax.ShapeDtypeStruct(xmid.shape, F32),
        compiler_params=pltpu.CompilerParams(
            dimension_semantics=("arbitrary",), vmem_limit_bytes=VMEM_LIMIT_BYTES),
        name="final_sample",
    )(xmid, ffn, mod, g)


def _slot_order(a, axis):
    shape = a.shape
    a = a.reshape(shape[:axis] + (N_EXPERT_GROUPS, GROUP_SIZE) + shape[axis + 1:])
    return jnp.swapaxes(a, axis, axis + 1).reshape(shape)


def _layer_weights(l, w_in, norm_mix_g, conv_a_w, conv_b_w, conv_b_bias, ln_b_g, ln_b_b, pool_w,
                   pool_scale, w_out, norm_ffn_g, w_router, router_bias, w_gate, w_up, w_down,
                   ws_gate, ws_up, ws_down):
    row = lambda v: v[l].reshape(1, -1)
    wr = _slot_order(w_router[l], 1).T
    wr_hi = wr.astype(BF16)
    blockdiag = jax.scipy.linalg.block_diag(*[pool_w[l, g] for g in range(pool_w.shape[1])])
    experts = lambda w: w[l].astype(BF16).reshape(
        (N_EXPERT_GROUPS, GROUP_SIZE) + w.shape[2:])
    precise = l + 1 < w_in.shape[0]
    stacked = lambda w: jnp.stack(_split(w)) if precise else w.astype(BF16)[None]
    return dict(
        norm_mix_g=row(norm_mix_g), w_in=stacked(w_in[l]), conv_a_w=conv_a_w[l],
        conv_b_w=conv_b_w[l], conv_b_bias=row(conv_b_bias), ln_b_g=row(ln_b_g), ln_b_b=row(ln_b_b),
        pool_w=stacked(blockdiag), pool_scale=row(pool_scale), w_out=stacked(w_out[l]),
        norm_ffn_g=row(norm_ffn_g), wr_hi=wr_hi, wr_lo=(wr - wr_hi.astype(F32)).astype(BF16),
        router_bias=_slot_order(router_bias[l], 0).reshape(N_EXPERTS, 1),
        w_gate=experts(w_gate), w_up=experts(w_up), w_down=experts(w_down),
        ws_gate=ws_gate[l].astype(BF16), ws_up=ws_up[l].astype(BF16), ws_down=ws_down[l].astype(BF16),
    )


def kernel(x_prompt, x_sample, c_prompt, c_sample, state_conv_a, state_conv_b, state_pool, w_ada, b_ada, norm_mix_g, w_in, conv_a_w, conv_b_w, conv_b_bias, ln_b_g, ln_b_b, pool_w, pool_scale, w_out, norm_ffn_g, w_router, router_bias, w_gate, w_up, w_down, ws_gate, ws_up, ws_down, final_norm_g):
    depth = w_ada.shape[0]
    n_p, seq, _ = x_prompt.shape
    n_s, t_new, _ = x_sample.shape

    mod = _ada(jnp.concatenate([c_prompt, c_sample], axis=0), w_ada, b_ada)
    mod_p = mod[:, :n_p].reshape(depth, n_p, 1, N_MOD * D_MODEL)
    mod_s = mod[:, n_p:]

    xp = x_prompt
    xs = jnp.swapaxes(x_sample, 0, 1)
    time_major = lambda st: jnp.swapaxes(st, 1, 2)
    st_a, st_b, st_c = time_major(state_conv_a), time_major(state_conv_b), time_major(state_pool)

    prev_p = prev_s = None
    new_p, new_s = [], []
    for l in range(depth):
        lw = _layer_weights(l, w_in, norm_mix_g, conv_a_w, conv_b_w, conv_b_bias, ln_b_g, ln_b_b,
                            pool_w, pool_scale, w_out, norm_ffn_g, w_router, router_bias,
                            w_gate, w_up, w_down, ws_gate, ws_up, ws_down)
        xp, h2p, ltp, na, nb, nc = _mixer_prompt(xp, prev_p, mod_p[l], lw)
        new_p.append((na, nb, nc))
        ffn_p = _moe(h2p.reshape(n_p * seq, D_MODEL), ltp, lw).reshape(n_p, seq, D_MODEL)
        prev_p = (ffn_p, mod_p[l])

        xs, h2s, lts, na, nb, nc = _mixer_sample(xs, prev_s, mod_s[l], (st_a[l], st_b[l], st_c[l]), lw)
        new_s.append((na, nb, nc))
        ffn_s = _moe(h2s, lts, lw).reshape(t_new, n_s, D_MODEL)
        prev_s = (ffn_s, mod_s[l])

    g = final_norm_g.reshape(1, D_MODEL)
    y_prompt = _final_prompt(xp, prev_p[0], prev_p[1], g)
    y_sample = jnp.swapaxes(_final_sample(xs, prev_s[0], prev_s[1], g), 0, 1)

    stack = lambda items, i: jnp.stack([it[i] for it in items])
    batch_major = lambda a: jnp.swapaxes(a, 1, 2)
    return (y_prompt, y_sample,
            stack(new_p, 0), stack(new_p, 1), stack(new_p, 2),
            batch_major(stack(new_s, 0)), batch_major(stack(new_s, 1)), batch_major(stack(new_s, 2)))
```

```python
import functools

import jax
import jax.numpy as jnp
from jax import lax
from jax.experimental import pallas as pl
from jax.experimental.pallas import tpu as pltpu

D_MODEL = 1024
HEAD_DIM = 64
D_A = 6 * HEAD_DIM
D_B = 6 * HEAD_DIM
D_C = D_MODEL - D_A - D_B
C_GROUP = 64
POOL_WINDOWS = (2, 4, 8, 16)
POOL_BUF = max(POOL_WINDOWS) - 1
CONV_A = 3
CONV_B = 31
D_IN = 3 * D_A + 2 * D_B + D_C
N_EXPERTS = 64
N_EXPERT_GROUPS = 8
GROUP_SIZE = N_EXPERTS // N_EXPERT_GROUPS
TOPK_GROUPS = 4
TOP_K = 8
D_EXPERT = 256
ROUTED_SCALE = 2.5
N_MOD = 6
EPS = 1e-6
PAST_LEN = 16384

F32 = jnp.float32
BF16 = jnp.bfloat16

VMEM_LIMIT_BYTES = 56 * 1024 * 1024

MIX_TILE = 512
HALO_A = 8
HALO_B = 32
HALO_C = 16

MOE_BLOCK = 256
MOE_CAP = 64
CODE_RADIX = 8
LANES = 128
MOE_SLOTS = 4
MOE_SUPER = 2048


def _silu(v):
    return v * jax.nn.sigmoid(v)


def _rms_mod(x, g, scale, shift):
    ms = jnp.mean(x * x, axis=-1, keepdims=True)
    y = (x * lax.rsqrt(ms + EPS)) * g
    return y * (1.0 + scale) + shift


def _split(v):
    hi = v.astype(BF16)
    return hi, (v - hi.astype(F32)).astype(BF16)


def _dot(a, w_hi, w_lo=None):
    a_hi, a_lo = _split(a)
    out = jnp.dot(a_hi, w_hi, preferred_element_type=F32)
    if w_lo is not None:
        out = out + jnp.dot(a_lo, w_hi, preferred_element_type=F32)
        out = out + jnp.dot(a_hi, w_lo, preferred_element_type=F32)
    return out


def _dot_ref(a, w_ref):
    return _dot(a, w_ref[0], w_ref[1] if w_ref.shape[0] == 2 else None)


def _mod_parts(mod):
    return [mod[..., i * D_MODEL:(i + 1) * D_MODEL] for i in range(N_MOD)]


def _ada_kernel(c_ref, w_ref, b_ref, o_ref):
    w_hi, w_lo = _split(w_ref[0])
    o_ref[0] = _dot(_silu(c_ref[...]), w_hi, w_lo) + b_ref[0]


def _ada(c_all, w_ada, b_ada):
    depth, _, n_out = w_ada.shape
    rows = c_all.shape[0]
    tn = 1536
    return pl.pallas_call(
        _ada_kernel,
        grid=(depth, n_out // tn),
        in_specs=[
            pl.BlockSpec((rows, D_MODEL), lambda l, n: (0, 0)),
            pl.BlockSpec((1, D_MODEL, tn), lambda l, n: (l, 0, n)),
            pl.BlockSpec((1, 1, tn), lambda l, n: (l, 0, n)),
        ],
        out_specs=pl.BlockSpec((1, rows, tn), lambda l, n: (l, 0, n)),
        out_shape=jax.ShapeDtypeStruct((depth, rows, n_out), F32),
        compiler_params=pltpu.CompilerParams(
            dimension_semantics=("arbitrary", "arbitrary"), vmem_limit_bytes=VMEM_LIMIT_BYTES),
        name="ada",
    )(c_all, w_ada, b_ada.reshape(depth, 1, n_out))


def _layer_norm(v, g, b):
    mu = jnp.mean(v, axis=-1, keepdims=True)
    d = v - mu
    var = jnp.mean(d * d, axis=-1, keepdims=True)
    return d * lax.rsqrt(var + EPS) * g + b


def _pool_select(sums, counts, p):
    lane = lax.broadcasted_iota(jnp.int32, p.shape, p.ndim - 1)
    pooled = sums[-1] / counts[-1]
    for gi in range(len(POOL_WINDOWS) - 2, -1, -1):
        pooled = jnp.where(lane < (gi + 1) * C_GROUP, sums[gi] / counts[gi], pooled)
    return pooled - p


def _router_logits_t(h2, h2_hi, wr_hi, wr_lo):
    h2_lo = (h2 - h2_hi.astype(F32)).astype(BF16)
    nt = (((1,), (1,)), ((), ()))
    lt = lax.dot_general(wr_hi, h2_hi, nt, preferred_element_type=F32)
    lt = lt + lax.dot_general(wr_hi, h2_lo, nt, preferred_element_type=F32)
    return lt + lax.dot_general(wr_lo, h2_hi, nt, preferred_element_type=F32)


def _mixer_prompt_kernel(has_prev, tile, *refs):
    if has_prev:
        x_ref, f_ref, modp_ref, *refs = refs
    else:
        x_ref, *refs = refs
    (mod_ref, ng_ref, win_ref, caw_ref, cbw_ref, cbb_ref, lng_ref, lnb_ref, pw_ref, ps_ref,
     wout_ref, nfg_ref, wrh_ref, wrl_ref,
     xmid_ref, h2_ref, lt_ref, na_ref, nb_ref, nc_ref,
     va_ref, ub_ref, pc_ref) = refs

    s = pl.program_id(1)

    @pl.when(s == 0)
    def _():
        va_ref[0:HALO_A, :] = jnp.zeros((HALO_A, D_A), F32)
        ub_ref[0:HALO_B, :] = jnp.zeros((HALO_B, D_B), F32)
        pc_ref[0:HALO_C, :] = jnp.zeros((HALO_C, D_C), F32)

    x = x_ref[0]
    if has_prev:
        x = x + _mod_parts(modp_ref[0])[5] * f_ref[0]
    sh_m, sc_m, g_m, sh_f, sc_f, g_f = _mod_parts(mod_ref[0])

    proj = _dot_ref(_rms_mod(x, ng_ref[...], sc_m, sh_m), win_ref)
    xa = proj[:, 0:D_A]
    ca = proj[:, D_A:2 * D_A]
    ba = proj[:, 2 * D_A:3 * D_A]
    a_b = proj[:, 3 * D_A:3 * D_A + D_B]
    g_b = proj[:, 3 * D_A + D_B:3 * D_A + 2 * D_B]
    p_c = proj[:, 3 * D_A + 2 * D_B:]

    va_ref[HALO_A:HALO_A + tile, :] = ca * xa
    acc = caw_ref[0:1, :] * va_ref[HALO_A - 2:HALO_A - 2 + tile, :]
    for k in range(1, CONV_A):
        acc = acc + caw_ref[k:k + 1, :] * va_ref[HALO_A - 2 + k:HALO_A - 2 + k + tile, :]
    ya = ba * acc
    na_ref[0] = va_ref[HALO_A + tile - (CONV_A - 1):HALO_A + tile, :]

    ub_ref[HALO_B:HALO_B + tile, :] = a_b * jax.nn.sigmoid(g_b)
    off = HALO_B - (CONV_B - 1)
    acc = cbw_ref[0:1, :] * ub_ref[off:off + tile, :]
    for k in range(1, CONV_B):
        acc = acc + cbw_ref[k:k + 1, :] * ub_ref[off + k:off + k + tile, :]
    yb = _silu(_layer_norm(acc + cbb_ref[...], lng_ref[...], lnb_ref[...]))
    nb_ref[0] = ub_ref[HALO_B + tile - (CONV_B - 1):HALO_B + tile, :]

    pc_ref[HALO_C:HALO_C + tile, :] = p_c
    pos = s * tile + lax.broadcasted_iota(jnp.int32, (tile, D_C), 0)
    run = p_c
    sums, counts = [], []
    for j in range(1, POOL_WINDOWS[-1]):
        run = run + pc_ref[HALO_C - j:HALO_C - j + tile, :]
        if j + 1 in POOL_WINDOWS:
            sums.append(run)
            counts.append(jnp.minimum(pos + 1, j + 1).astype(F32))
    d = _pool_select(sums, counts, p_c)
    yc = _dot_ref(d, pw_ref) * ps_ref[...]
    nc_ref[0] = pc_ref[HALO_C + tile - POOL_BUF:HALO_C + tile, :]

    va_ref[0:HALO_A, :] = va_ref[tile:tile + HALO_A, :]
    ub_ref[0:HALO_B, :] = ub_ref[tile:tile + HALO_B, :]
    pc_ref[0:HALO_C, :] = pc_ref[tile:tile + HALO_C, :]

    mix = jnp.concatenate([ya, yb, yc], axis=-1)
    xmid = x + g_m * _dot_ref(mix, wout_ref)
    xmid_ref[0] = xmid

    h2 = _rms_mod(xmid, nfg_ref[...], sc_f, sh_f)
    h2_hi = h2.astype(BF16)
    h2_ref[0] = h2_hi
    lt_ref[...] = _router_logits_t(h2, h2_hi, wrh_ref[...], wrl_ref[...])


def _full(shape):
    return pl.BlockSpec(shape, lambda *_: (0,) * len(shape))


def _mixer_prompt(x, prev, mod, lw):
    nb, seq, _ = x.shape
    tile = min(MIX_TILE, seq)
    nt = seq // tile
    tok = pl.BlockSpec((1, tile, D_MODEL), lambda b, s: (b, s, 0))
    modspec = pl.BlockSpec((1, 1, N_MOD * D_MODEL), lambda b, s: (b, 0, 0))
    args, specs = [x], [tok]
    if prev is not None:
        args += [prev[0], prev[1]]
        specs += [tok, modspec]
    args += [mod, lw["norm_mix_g"], lw["w_in"], lw["conv_a_w"], lw["conv_b_w"], lw["conv_b_bias"],
             lw["ln_b_g"], lw["ln_b_b"], lw["pool_w"], lw["pool_scale"], lw["w_out"],
             lw["norm_ffn_g"], lw["wr_hi"], lw["wr_lo"]]
    specs += [modspec] + [_full(a.shape) for a in args[len(specs) + 1:]]
    out_shape = (
        jax.ShapeDtypeStruct((nb, seq, D_MODEL), F32),
        jax.ShapeDtypeStruct((nb, seq, D_MODEL), BF16),
        jax.ShapeDtypeStruct((N_EXPERTS, nb * seq), F32),
        jax.ShapeDtypeStruct((nb, CONV_A - 1, D_A), F32),
        jax.ShapeDtypeStruct((nb, CONV_B - 1, D_B), F32),
        jax.ShapeDtypeStruct((nb, POOL_BUF, D_C), F32),
    )
    out_specs = (
        tok, tok,
        pl.BlockSpec((N_EXPERTS, tile), lambda b, s: (0, b * nt + s)),
        pl.BlockSpec((1, CONV_A - 1, D_A), lambda b, s: (b, 0, 0)),
        pl.BlockSpec((1, CONV_B - 1, D_B), lambda b, s: (b, 0, 0)),
        pl.BlockSpec((1, POOL_BUF, D_C), lambda b, s: (b, 0, 0)),
    )
    return pl.pallas_call(
        functools.partial(_mixer_prompt_kernel, prev is not None, tile),
        grid=(nb, nt),
        in_specs=specs,
        out_specs=out_specs,
        out_shape=out_shape,
        scratch_shapes=[
            pltpu.VMEM((HALO_A + tile, D_A), F32),
            pltpu.VMEM((HALO_B + tile, D_B), F32),
            pltpu.VMEM((HALO_C + tile, D_C), F32),
        ],
        compiler_params=pltpu.CompilerParams(
            dimension_semantics=("arbitrary", "arbitrary"), vmem_limit_bytes=VMEM_LIMIT_BYTES),
        name="mixer_prompt",
    )(*args)


def _mixer_sample_kernel(has_prev, *refs):
    if has_prev:
        x_ref, f_ref, modp_ref, *refs = refs
    else:
        x_ref, *refs = refs
    (mod_ref, sa_ref, sb_ref, sc_ref, ng_ref, win_ref, caw_ref, cbw_ref, cbb_ref, lng_ref, lnb_ref,
     pw_ref, ps_ref, wout_ref, nfg_ref, wrh_ref, wrl_ref,
     xmid_ref, h2_ref, lt_ref, na_ref, nb_ref, nc_ref) = refs

    t_new, n_seq, _ = x_ref.shape
    rows = t_new * n_seq
    x = x_ref[...]
    if has_prev:
        x = x + _mod_parts(modp_ref[...])[5][None] * f_ref[...]
    sh_m, sc_m, g_m, sh_f, sc_f, g_f = [m[None] for m in _mod_parts(mod_ref[...])]

    h = _rms_mod(x, ng_ref[...], sc_m, sh_m)
    proj = _dot_ref(h.reshape(rows, D_MODEL), win_ref).reshape(t_new, n_seq, D_IN)
    xa = proj[:, :, 0:D_A]
    ca = proj[:, :, D_A:2 * D_A]
    ba = proj[:, :, 2 * D_A:3 * D_A]
    a_b = proj[:, :, 3 * D_A:3 * D_A + D_B]
    g_b = proj[:, :, 3 * D_A + D_B:3 * D_A + 2 * D_B]
    p_c = proj[:, :, 3 * D_A + 2 * D_B:]

    def history(state_ref, new):
        return [state_ref[j] for j in range(state_ref.shape[0])] + [new[t] for t in range(t_new)]

    def conv(seq, w_ref, taps):
        outs = []
        for t in range(t_new):
            acc = w_ref[0:1, :] * seq[t]
            for k in range(1, taps):
                acc = acc + w_ref[k:k + 1, :] * seq[t + k]
            outs.append(acc)
        return jnp.stack(outs)

    def emit_state(out_ref, seq):
        keep = out_ref.shape[0]
        for j in range(keep):
            out_ref[j] = seq[len(seq) - keep + j]

    seq_a = history(sa_ref, ca * xa)
    ya = ba * conv(seq_a, caw_ref, CONV_A)
    emit_state(na_ref, seq_a)

    seq_b = history(sb_ref, a_b * jax.nn.sigmoid(g_b))
    vb = conv(seq_b, cbw_ref, CONV_B) + cbb_ref[...]
    yb = _silu(_layer_norm(vb, lng_ref[...], lnb_ref[...]))
    emit_state(nb_ref, seq_b)

    seq_c = history(sc_ref, p_c)
    pooled = []
    for t in range(t_new):
        cur = POOL_BUF + t
        run = seq_c[cur]
        sums, counts = [], []
        for j in range(1, POOL_WINDOWS[-1]):
            run = run + seq_c[cur - j]
            if j + 1 in POOL_WINDOWS:
                sums.append(run)
                counts.append(float(min(PAST_LEN + t + 1, j + 1)))
        pooled.append(_pool_select(sums, counts, seq_c[cur]))
    d = jnp.stack(pooled)
    yc = _dot_ref(d.reshape(rows, D_C), pw_ref).reshape(t_new, n_seq, D_C) * ps_ref[...]
    emit_state(nc_ref, seq_c)

    mix = jnp.concatenate([ya, yb, yc], axis=-1).reshape(rows, D_MODEL)
    xmid = x + g_m * _dot_ref(mix, wout_ref).reshape(t_new, n_seq, D_MODEL)
    xmid_ref[...] = xmid

    h2 = _rms_mod(xmid, nfg_ref[...], sc_f, sh_f).reshape(rows, D_MODEL)
    h2_hi = h2.astype(BF16)
    h2_ref[...] = h2_hi
    lt_ref[...] = _router_logits_t(h2, h2_hi, wrh_ref[...], wrl_ref[...])


def _mixer_sample(x, prev, mod, states, lw):
    t_new, n_seq, _ = x.shape
    rows = t_new * n_seq
    args = [x]
    if prev is not None:
        args += [prev[0], prev[1]]
    args += [mod, *states, lw["norm_mix_g"], lw["w_in"], lw["conv_a_w"], lw["conv_b_w"],
             lw["conv_b_bias"], lw["ln_b_g"], lw["ln_b_b"], lw["pool_w"], lw["pool_scale"],
             lw["w_out"], lw["norm_ffn_g"], lw["wr_hi"], lw["wr_lo"]]
    out_shape = (
        jax.ShapeDtypeStruct((t_new, n_seq, D_MODEL), F32),
        jax.ShapeDtypeStruct((rows, D_MODEL), BF16),
        jax.ShapeDtypeStruct((N_EXPERTS, rows), F32),
        jax.ShapeDtypeStruct((CONV_A - 1, n_seq, D_A), F32),
        jax.ShapeDtypeStruct((CONV_B - 1, n_seq, D_B), F32),
        jax.ShapeDtypeStruct((POOL_BUF, n_seq, D_C), F32),
    )
    return pl.pallas_call(
        functools.partial(_mixer_sample_kernel, prev is not None),
        grid=(1,),
        in_specs=[_full(a.shape) for a in args],
        out_specs=tuple(_full(o.shape) for o in out_shape),
        out_shape=out_shape,
        compiler_params=pltpu.CompilerParams(
            dimension_semantics=("arbitrary",), vmem_limit_bytes=VMEM_LIMIT_BYTES),
        name="mixer_sample",
    )(*args)


def _route(lt, bias):
    n_tok = lt.shape[1]
    scores = jax.nn.sigmoid(lt)
    biased = scores + bias
    sj = [scores[GROUP_SIZE * j:GROUP_SIZE * (j + 1)] for j in range(GROUP_SIZE)]
    bj = [biased[GROUP_SIZE * j:GROUP_SIZE * (j + 1)] for j in range(GROUP_SIZE)]

    m1 = bj[0]
    m2 = jnp.full_like(m1, -jnp.inf)
    for j in range(1, GROUP_SIZE):
        m2 = jnp.maximum(m2, jnp.minimum(m1, bj[j]))
        m1 = jnp.maximum(m1, bj[j])
    gscore = m1 + m2

    gidx = lax.broadcasted_iota(jnp.int32, (N_EXPERT_GROUPS, n_tok), 0)
    lower = [None] + [jnp.where(gidx >= k, gidx - k, gidx - k + N_EXPERT_GROUPS) < gidx
                      for k in range(1, N_EXPERT_GROUPS)]

    def beats(other, mine, tie_wins):
        return (other > mine) | ((other == mine) & tie_wins)

    grank = jnp.zeros((N_EXPERT_GROUPS, n_tok), jnp.int32)
    for k in range(1, N_EXPERT_GROUPS):
        grank = grank + beats(pltpu.roll(gscore, k, 0), gscore, lower[k]).astype(jnp.int32)
    gsel = grank < TOPK_GROUPS

    def over_groups(op, v):
        for k in (4, 2, 1):
            v = op(v, pltpu.roll(v, k, 0))
        return v

    mj = [jnp.where(gsel, b, -jnp.inf) for b in bj]
    eid = [(gidx * GROUP_SIZE + j).astype(F32) for j in range(GROUP_SIZE)]
    sel = [jnp.zeros((N_EXPERT_GROUPS, n_tok), F32) for _ in range(GROUP_SIZE)]
    for _ in range(TOP_K):
        top = over_groups(jnp.maximum, functools.reduce(jnp.maximum, mj))
        first = over_groups(jnp.minimum, functools.reduce(
            jnp.minimum,
            [jnp.where(mj[j] == top, eid[j], float(N_EXPERTS)) for j in range(GROUP_SIZE)]))
        for j in range(GROUP_SIZE):
            hit = eid[j] == first
            sel[j] = jnp.where(hit, 1.0, sel[j])
            mj[j] = jnp.where(hit, -jnp.inf, mj[j])

    picked = [sel[j] * sj[j] for j in range(GROUP_SIZE)]
    tot = picked[0]
    for j in range(1, GROUP_SIZE):
        tot = tot + picked[j]
    denom = jnp.sum(tot, axis=0, keepdims=True)
    gates = [picked[j] / denom * ROUTED_SCALE for j in range(GROUP_SIZE)]
    return jnp.concatenate(sel, axis=0), jnp.concatenate(gates, axis=0)


def _moe_kernel(n_blocks, h2_ref, lt_ref, rb_ref, wg_ref, wu_ref, wd_ref, wsg_ref, wsu_ref,
                wsd_ref, o_ref, rank_ref, gate_ref, p_ref, xs_ref, ys_ref, gs_ref, codes_ref):
    q = pl.program_id(1)
    seg = MOE_CAP
    blk = MOE_BLOCK

    @pl.when(q == 0)
    def _():
        sel, gates = _route(lt_ref[...], rb_ref[...])
        gate_ref[...] = gates
        before = jnp.where(lax.broadcasted_iota(jnp.int32, (blk, blk), 0)
                           < lax.broadcasted_iota(jnp.int32, (blk, blk), 1), 1.0, 0.0).astype(BF16)
        passes = []
        for b in range(n_blocks):
            sb = sel[:, b * blk:(b + 1) * blk]
            cnt = jnp.dot(sb.astype(BF16), before, preferred_element_type=F32)
            rank_ref[:, b * blk:(b + 1) * blk] = jnp.where(sb > 0.5, cnt, -1.0)
            total = jnp.broadcast_to(jnp.max((cnt + 1.0) * sb, axis=1, keepdims=True),
                                     (N_EXPERTS, LANES))
            passes.append(((total + (seg - 1)) * (1.0 / seg)).astype(jnp.int32))
        row = lax.broadcasted_iota(jnp.int32, (N_EXPERT_GROUPS, LANES), 0) % MOE_SLOTS
        field = functools.reduce(lambda a, s: jnp.where(row == s, CODE_RADIX ** s, a),
                                 range(1, MOE_SLOTS), jnp.ones_like(row))

        def over_slots(op, v):
            v = op(v, pltpu.roll(v, 1, 0))
            return op(v, pltpu.roll(v, 2, 0))

        for j in range(GROUP_SIZE):
            slab = [p[N_EXPERT_GROUPS * j:N_EXPERT_GROUPS * (j + 1)] for p in passes]
            slot_code = over_slots(jnp.add, functools.reduce(jnp.maximum, slab) * field)
            blk_code = functools.reduce(
                jnp.add, [over_slots(jnp.maximum, slab[b]) * CODE_RADIX ** b for b in range(n_blocks)])
            for i in range(N_EXPERT_GROUPS // MOE_SLOTS):
                last = MOE_SLOTS * (i + 1) - 1
                step = (N_EXPERT_GROUPS // MOE_SLOTS) * j + i
                codes_ref[step] = jnp.max(slot_code[last:last + 1])
                codes_ref[N_EXPERTS // MOE_SLOTS + step] = jnp.max(blk_code[last:last + 1])
        x = h2_ref[...]
        hid = _silu(jnp.dot(x, wsg_ref[...], preferred_element_type=F32)) * jnp.dot(
            x, wsu_ref[...], preferred_element_type=F32)
        o_ref[...] = jnp.dot(hid.astype(BF16), wsd_ref[...], preferred_element_type=F32)

    r0 = q * MOE_SLOTS
    pos = lax.broadcasted_iota(jnp.int32, (seg, blk), 0).astype(F32)

    def passes_of(code, i):
        return lax.shift_right_logical(code, 3 * i) & (CODE_RADIX - 1)

    slot_passes = [passes_of(codes_ref[q], s) for s in range(MOE_SLOTS)]
    blk_passes = [passes_of(codes_ref[N_EXPERTS // MOE_SLOTS + q], b) for b in range(n_blocks)]

    def gather(b, p):
        base = (p * seg).astype(F32) if not isinstance(p, int) else float(p * seg)
        rows = []
        for s in range(MOE_SLOTS):
            rrow = rank_ref[pl.ds(r0 + s, 1), b * blk:(b + 1) * blk]
            grow = gate_ref[pl.ds(r0 + s, 1), b * blk:(b + 1) * blk]
            match = (rrow - base) == pos
            rows.append(jnp.where(match, 1.0, 0.0).astype(BF16))
            gs_ref[s, b] = jnp.sum(jnp.where(match, grow, 0.0), axis=1, keepdims=True)
        onehot = jnp.concatenate(rows, axis=0)
        p_ref[b] = onehot
        xs = jnp.dot(onehot, h2_ref[b * blk:(b + 1) * blk, :],
                     preferred_element_type=F32).astype(BF16)
        for s in range(MOE_SLOTS):
            xs_ref[s, b] = xs[s * seg:(s + 1) * seg]

    def expert_mlp(s):
        lhs = xs_ref[s].reshape(n_blocks * seg, D_MODEL)
        hid = _silu(jnp.dot(lhs, wg_ref[s, 0], preferred_element_type=F32)) * jnp.dot(
            lhs, wu_ref[s, 0], preferred_element_type=F32)
        y = jnp.dot(hid.astype(BF16), wd_ref[s, 0], preferred_element_type=F32)
        y = gs_ref[s].reshape(n_blocks * seg, 1) * y
        ys_ref[s] = y.astype(BF16).reshape(n_blocks, seg, D_MODEL)

    def combine(b):
        yb = jnp.concatenate([ys_ref[s, b] for s in range(MOE_SLOTS)], axis=0)
        o_ref[b * blk:(b + 1) * blk, :] += lax.dot_general(
            p_ref[b], yb, (((0,), (0,)), ((), ())), preferred_element_type=F32)

    for b in range(n_blocks):
        gather(b, 0)
    for s in range(MOE_SLOTS):
        expert_mlp(s)
    for b in range(n_blocks):
        combine(b)

    def overflow_pass(p, carry):
        for b in range(n_blocks):
            pl.when(blk_passes[b] > p)(functools.partial(gather, b, p))
        for s in range(MOE_SLOTS):
            pl.when(slot_passes[s] > p)(functools.partial(expert_mlp, s))
        for b in range(n_blocks):
            pl.when(blk_passes[b] > p)(functools.partial(combine, b))
        return carry

    lax.fori_loop(1, functools.reduce(jnp.maximum, slot_passes), overflow_pass, 0)


def _moe(h2, lt, lw):
    n_tok = h2.shape[0]
    sup = min(MOE_SUPER, n_tok)
    n_blocks = sup // MOE_BLOCK
    n_steps = N_EXPERTS // MOE_SLOTS
    half = N_EXPERT_GROUPS // MOE_SLOTS
    layer = lw["layer"]
    assert n_blocks * 3 <= 30, "block pass counts are packed as 3-bit fields of one int32"
    wspec = lambda a, b: pl.BlockSpec(
        (None, MOE_SLOTS, 1, a, b), lambda t, q: (layer, q % half, q // half, 0, 0))
    return pl.pallas_call(
        functools.partial(_moe_kernel, n_blocks),
        grid=(n_tok // sup, n_steps),
        in_specs=[
            pl.BlockSpec((sup, D_MODEL), lambda t, q: (t, 0)),
            pl.BlockSpec((N_EXPERTS, sup), lambda t, q: (0, t)),
            _full((N_EXPERTS, 1)),
            wspec(D_MODEL, D_EXPERT), wspec(D_MODEL, D_EXPERT), wspec(D_EXPERT, D_MODEL),
            _full(lw["ws_gate"].shape), _full(lw["ws_up"].shape), _full(lw["ws_down"].shape),
        ],
        out_specs=pl.BlockSpec((sup, D_MODEL), lambda t, q: (t, 0)),
        out_shape=jax.ShapeDtypeStruct((n_tok, D_MODEL), F32),
        scratch_shapes=[
            pltpu.VMEM((N_EXPERTS, sup), F32),
            pltpu.VMEM((N_EXPERTS, sup), F32),
            pltpu.VMEM((n_blocks, MOE_SLOTS * MOE_CAP, MOE_BLOCK), BF16),
            pltpu.VMEM((MOE_SLOTS, n_blocks, MOE_CAP, D_MODEL), BF16),
            pltpu.VMEM((MOE_SLOTS, n_blocks, MOE_CAP, D_MODEL), BF16),
            pltpu.VMEM((MOE_SLOTS, n_blocks, MOE_CAP, 1), F32),
            pltpu.SMEM((2 * n_steps,), jnp.int32),
        ],
        compiler_params=pltpu.CompilerParams(
            dimension_semantics=("arbitrary", "arbitrary"), vmem_limit_bytes=VMEM_LIMIT_BYTES),
        name="moe",
    )(h2, lt, lw["router_bias"], lw["w_gate"], lw["w_up"], lw["w_down"],
      lw["ws_gate"], lw["ws_up"], lw["ws_down"])


def _final_kernel(x_ref, f_ref, mod_ref, g_ref, o_ref):
    gate = _mod_parts(mod_ref[...])[5]
    if x_ref.ndim == 3 and mod_ref.ndim == 2:
        gate = gate[None]
    x = x_ref[...] + gate * f_ref[...]
    ms = jnp.mean(x * x, axis=-1, keepdims=True)
    o_ref[...] = (x * lax.rsqrt(ms + EPS)) * g_ref[...]


def _final_prompt(xmid, ffn, mod, g):
    nb, seq, _ = xmid.shape
    tile = min(MIX_TILE, seq)
    tok = pl.BlockSpec((1, tile, D_MODEL), lambda b, s: (b, s, 0))
    return pl.pallas_call(
        _final_kernel,
        grid=(nb, seq // tile),
        in_specs=[tok, tok, pl.BlockSpec((1, 1, N_MOD * D_MODEL), lambda b, s: (b, 0, 0)),
                  _full(g.shape)],
        out_specs=tok,
        out_shape=jax.ShapeDtypeStruct(xmid.shape, F32),
        compiler_params=pltpu.CompilerParams(
            dimension_semantics=("arbitrary", "arbitrary"), vmem_limit_bytes=VMEM_LIMIT_BYTES),
        name="final_prompt",
    )(xmid, ffn, mod, g)


def _final_sample(xmid, ffn, mod, g):
    return pl.pallas_call(
        _final_kernel,
        grid=(1,),
        in_specs=[_full(xmid.shape), _full(ffn.shape), _full(mod.shape), _full(g.shape)],
        out_specs=_full(xmid.shape),
        out_shape=jax.ShapeDtypeStruct(xmid.shape, F32),
        compiler_params=pltpu.CompilerParams(
            dimension_semantics=("arbitrary",), vmem_limit_bytes=VMEM_LIMIT_BYTES),
        name="final_sample",
    )(xmid, ffn, mod, g)


def _slot_order(a, axis):
    shape = a.shape
    a = a.reshape(shape[:axis] + (N_EXPERT_GROUPS, GROUP_SIZE) + shape[axis + 1:])
    return jnp.swapaxes(a, axis, axis + 1).reshape(shape)


def _layer_weights(l, w_in, norm_mix_g, conv_a_w, conv_b_w, conv_b_bias, ln_b_g, ln_b_b, pool_w,
                   pool_scale, w_out, norm_ffn_g, w_router, router_bias, w_gate, w_up, w_down,
                   ws_gate, ws_up, ws_down):
    row = lambda v: v[l].reshape(1, -1)
    wr = _slot_order(w_router[l], 1).T
    wr_hi = wr.astype(BF16)
    blockdiag = jax.scipy.linalg.block_diag(*[pool_w[l, g] for g in range(pool_w.shape[1])])
    precise = l + 1 < w_in.shape[0]
    stacked = lambda w: jnp.stack(_split(w)) if precise else w.astype(BF16)[None]
    return dict(
        norm_mix_g=row(norm_mix_g), w_in=stacked(w_in[l]), conv_a_w=conv_a_w[l],
        conv_b_w=conv_b_w[l], conv_b_bias=row(conv_b_bias), ln_b_g=row(ln_b_g), ln_b_b=row(ln_b_b),
        pool_w=stacked(blockdiag), pool_scale=row(pool_scale), w_out=stacked(w_out[l]),
        norm_ffn_g=row(norm_ffn_g), wr_hi=wr_hi, wr_lo=(wr - wr_hi.astype(F32)).astype(BF16),
        router_bias=_slot_order(router_bias[l], 0).reshape(N_EXPERTS, 1),
        layer=l, w_gate=w_gate, w_up=w_up, w_down=w_down,
        ws_gate=ws_gate[l].astype(BF16), ws_up=ws_up[l].astype(BF16), ws_down=ws_down[l].astype(BF16),
    )


def kernel(x_prompt, x_sample, c_prompt, c_sample, state_conv_a, state_conv_b, state_pool, w_ada, b_ada, norm_mix_g, w_in, conv_a_w, conv_b_w, conv_b_bias, ln_b_g, ln_b_b, pool_w, pool_scale, w_out, norm_ffn_g, w_router, router_bias, w_gate, w_up, w_down, ws_gate, ws_up, ws_down, final_norm_g):
    depth = w_ada.shape[0]
    n_p, seq, _ = x_prompt.shape
    n_s, t_new, _ = x_sample.shape

    mod = _ada(jnp.concatenate([c_prompt, c_sample], axis=0), w_ada, b_ada)
    mod_p = mod[:, :n_p].reshape(depth, n_p, 1, N_MOD * D_MODEL)
    mod_s = mod[:, n_p:]

    xp = x_prompt
    xs = jnp.swapaxes(x_sample, 0, 1)
    time_major = lambda st: jnp.swapaxes(st, 1, 2)
    st_a, st_b, st_c = time_major(state_conv_a), time_major(state_conv_b), time_major(state_pool)

    experts = lambda w: w.astype(BF16).reshape((depth, N_EXPERT_GROUPS, GROUP_SIZE) + w.shape[2:])
    w_gate, w_up, w_down = experts(w_gate), experts(w_up), experts(w_down)

    prev_p = prev_s = None
    new_p, new_s = [], []
    for l in range(depth):
        lw = _layer_weights(l, w_in, norm_mix_g, conv_a_w, conv_b_w, conv_b_bias, ln_b_g, ln_b_b,
                            pool_w, pool_scale, w_out, norm_ffn_g, w_router, router_bias,
                            w_gate, w_up, w_down, ws_gate, ws_up, ws_down)
        xp, h2p, ltp, na, nb, nc = _mixer_prompt(xp, prev_p, mod_p[l], lw)
        new_p.append((na, nb, nc))
        ffn_p = _moe(h2p.reshape(n_p * seq, D_MODEL), ltp, lw).reshape(n_p, seq, D_MODEL)
        prev_p = (ffn_p, mod_p[l])

        xs, h2s, lts, na, nb, nc = _mixer_sample(xs, prev_s, mod_s[l], (st_a[l], st_b[l], st_c[l]), lw)
        new_s.append((na, nb, nc))
        ffn_s = _moe(h2s, lts, lw).reshape(t_new, n_s, D_MODEL)
        prev_s = (ffn_s, mod_s[l])

    g = final_norm_g.reshape(1, D_MODEL)
    y_prompt = _final_prompt(xp, prev_p[0], prev_p[1], g)
    y_sample = jnp.swapaxes(_final_sample(xs, prev_s[0], prev_s[1], g), 0, 1)

    stack = lambda items, i: jnp.stack([it[i] for it in items])
    batch_major = lambda a: jnp.swapaxes(a, 1, 2)
    return (y_prompt, y_sample,
            stack(new_p, 0), stack(new_p, 1), stack(new_p, 2),
            batch_major(stack(new_s, 0)), batch_major(stack(new_s, 1)), batch_major(stack(new_s, 2)))
```

```python
import functools

import jax
import jax.numpy as jnp
from jax import lax
from jax.experimental import pallas as pl
from jax.experimental.pallas import tpu as pltpu

D_MODEL = 1024
HEAD_DIM = 64
D_A = 6 * HEAD_DIM
D_B = 6 * HEAD_DIM
D_C = D_MODEL - D_A - D_B
C_GROUP = 64
POOL_WINDOWS = (2, 4, 8, 16)
POOL_BUF = max(POOL_WINDOWS) - 1
CONV_A = 3
CONV_B = 31
D_IN = 3 * D_A + 2 * D_B + D_C
N_EXPERTS = 64
N_EXPERT_GROUPS = 8
GROUP_SIZE = N_EXPERTS // N_EXPERT_GROUPS
TOPK_GROUPS = 4
TOP_K = 8
D_EXPERT = 256
ROUTED_SCALE = 2.5
N_MOD = 6
EPS = 1e-6
PAST_LEN = 16384

F32 = jnp.float32
BF16 = jnp.bfloat16

VMEM_LIMIT_BYTES = 56 * 1024 * 1024

MIX_TILE = 512
HALO_A = 8
HALO_B = 32
HALO_C = 16

MOE_BLOCK = 256
MOE_CAP = 64
CODE_RADIX = 8
LANES = 128
SUBLANES = 8
MOE_SLOTS = 4
MOE_SUPER = 2048


def _silu(v):
    return v * jax.nn.sigmoid(v)


def _rms_mod(x, g, scale, shift):
    ms = jnp.mean(x * x, axis=-1, keepdims=True)
    y = (x * lax.rsqrt(ms + EPS)) * g
    return y * (1.0 + scale) + shift


def _split(v):
    hi = v.astype(BF16)
    return hi, (v - hi.astype(F32)).astype(BF16)


def _dot(a, w_hi, w_lo=None):
    a_hi, a_lo = _split(a)
    out = jnp.dot(a_hi, w_hi, preferred_element_type=F32)
    if w_lo is not None:
        out = out + jnp.dot(a_lo, w_hi, preferred_element_type=F32)
        out = out + jnp.dot(a_hi, w_lo, preferred_element_type=F32)
    return out


def _dot_ref(a, w_ref):
    return _dot(a, w_ref[0], w_ref[1] if w_ref.shape[0] == 2 else None)


def _mod_parts(mod):
    return [mod[..., i * D_MODEL:(i + 1) * D_MODEL] for i in range(N_MOD)]


def _ada_kernel(c_ref, w_ref, b_ref, o_ref):
    w_hi, w_lo = _split(w_ref[0])
    o_ref[0] = _dot(_silu(c_ref[...]), w_hi, w_lo) + b_ref[0]


def _ada(c_all, w_ada, b_ada):
    depth, _, n_out = w_ada.shape
    rows = c_all.shape[0]
    tn = 1536
    return pl.pallas_call(
        _ada_kernel,
        grid=(depth, n_out // tn),
        in_specs=[
            pl.BlockSpec((rows, D_MODEL), lambda l, n: (0, 0)),
            pl.BlockSpec((1, D_MODEL, tn), lambda l, n: (l, 0, n)),
            pl.BlockSpec((1, 1, tn), lambda l, n: (l, 0, n)),
        ],
        out_specs=pl.BlockSpec((1, rows, tn), lambda l, n: (l, 0, n)),
        out_shape=jax.ShapeDtypeStruct((depth, rows, n_out), F32),
        compiler_params=pltpu.CompilerParams(
            dimension_semantics=("arbitrary", "arbitrary"), vmem_limit_bytes=VMEM_LIMIT_BYTES),
        name="ada",
    )(c_all, w_ada, b_ada.reshape(depth, 1, n_out))


def _layer_norm(v, g, b):
    mu = jnp.mean(v, axis=-1, keepdims=True)
    d = v - mu
    var = jnp.mean(d * d, axis=-1, keepdims=True)
    return d * lax.rsqrt(var + EPS) * g + b


def _pool_select(sums, counts, p):
    lane = lax.broadcasted_iota(jnp.int32, p.shape, p.ndim - 1)
    pooled = sums[-1] / counts[-1]
    for gi in range(len(POOL_WINDOWS) - 2, -1, -1):
        pooled = jnp.where(lane < (gi + 1) * C_GROUP, sums[gi] / counts[gi], pooled)
    return pooled - p


def _router_logits_t(h2, h2_hi, wr_hi, wr_lo):
    h2_lo = (h2 - h2_hi.astype(F32)).astype(BF16)
    nt = (((1,), (1,)), ((), ()))
    both = lax.dot_general(jnp.concatenate([wr_hi, wr_lo], axis=0), h2_hi, nt,
                           preferred_element_type=F32)
    lt = both[:N_EXPERTS] + lax.dot_general(wr_hi, h2_lo, nt, preferred_element_type=F32)
    return lt + both[N_EXPERTS:]


def _mixer_prompt_kernel(has_prev, tile, *refs):
    if has_prev:
        x_ref, f_ref, modp_ref, *refs = refs
    else:
        x_ref, *refs = refs
    (mod_ref, ng_ref, win_ref, caw_ref, cbw_ref, cbb_ref, lng_ref, lnb_ref, pw_ref, ps_ref,
     wout_ref, nfg_ref, wrh_ref, wrl_ref,
     xmid_ref, h2_ref, lt_ref, na_ref, nb_ref, nc_ref,
     va_ref, ub_ref, pc_ref, ubs_ref, pcs_ref) = refs

    s = pl.program_id(1)

    @pl.when(s == 0)
    def _():
        va_ref[0:HALO_A, :] = jnp.zeros((HALO_A, D_A), F32)
        ub_ref[0:HALO_B, :] = jnp.zeros((HALO_B, D_B), F32)
        pc_ref[0:HALO_C, :] = jnp.zeros((HALO_C, D_C), F32)

    x = x_ref[0]
    if has_prev:
        x = x + _mod_parts(modp_ref[0])[5] * f_ref[0]
    sh_m, sc_m, g_m, sh_f, sc_f, g_f = _mod_parts(mod_ref[0])

    proj = _dot_ref(_rms_mod(x, ng_ref[...], sc_m, sh_m), win_ref)
    xa = proj[:, 0:D_A]
    ca = proj[:, D_A:2 * D_A]
    ba = proj[:, 2 * D_A:3 * D_A]
    a_b = proj[:, 3 * D_A:3 * D_A + D_B]
    g_b = proj[:, 3 * D_A + D_B:3 * D_A + 2 * D_B]
    p_c = proj[:, 3 * D_A + 2 * D_B:]

    va_ref[HALO_A:HALO_A + tile, :] = ca * xa
    acc = caw_ref[0:1, :] * va_ref[HALO_A - 2:HALO_A - 2 + tile, :]
    for k in range(1, CONV_A):
        acc = acc + caw_ref[k:k + 1, :] * va_ref[HALO_A - 2 + k:HALO_A - 2 + k + tile, :]
    ya = ba * acc
    na_ref[0] = va_ref[HALO_A + tile - (CONV_A - 1):HALO_A + tile, :]

    def window_reader(src_ref, shifted_ref):
        span = shifted_ref.shape[1]
        for r in range(1, SUBLANES):
            shifted_ref[r - 1] = src_ref[r:r + span, :]

        def window(row):
            r = row % SUBLANES
            if r == 0:
                return src_ref[row:row + tile, :]
            return shifted_ref[r - 1, row - r:row - r + tile, :]
        return window

    ub_ref[HALO_B:HALO_B + tile, :] = a_b * jax.nn.sigmoid(g_b)
    window = window_reader(ub_ref, ubs_ref)
    off = HALO_B - (CONV_B - 1)
    acc = cbw_ref[0:1, :] * window(off)
    for k in range(1, CONV_B):
        acc = acc + cbw_ref[k:k + 1, :] * window(off + k)
    yb = _silu(_layer_norm(acc + cbb_ref[...], lng_ref[...], lnb_ref[...]))
    nb_ref[0] = ub_ref[HALO_B + tile - (CONV_B - 1):HALO_B + tile, :]

    pc_ref[HALO_C:HALO_C + tile, :] = p_c
    window = window_reader(pc_ref, pcs_ref)
    pos = s * tile + lax.broadcasted_iota(jnp.int32, (tile, D_C), 0)
    run = p_c
    sums, counts = [], []
    for j in range(1, POOL_WINDOWS[-1]):
        run = run + window(HALO_C - j)
        if j + 1 in POOL_WINDOWS:
            sums.append(run)
            counts.append(jnp.minimum(pos + 1, j + 1).astype(F32))
    d = _pool_select(sums, counts, p_c)
    yc = _dot_ref(d, pw_ref) * ps_ref[...]
    nc_ref[0] = pc_ref[HALO_C + tile - POOL_BUF:HALO_C + tile, :]

    va_ref[0:HALO_A, :] = va_ref[tile:tile + HALO_A, :]
    ub_ref[0:HALO_B, :] = ub_ref[tile:tile + HALO_B, :]
    pc_ref[0:HALO_C, :] = pc_ref[tile:tile + HALO_C, :]

    mix = jnp.concatenate([ya, yb, yc], axis=-1)
    xmid = x + g_m * _dot_ref(mix, wout_ref)
    xmid_ref[0] = xmid

    h2 = _rms_mod(xmid, nfg_ref[...], sc_f, sh_f)
    h2_hi = h2.astype(BF16)
    h2_ref[0] = h2_hi
    lt_ref[...] = _router_logits_t(h2, h2_hi, wrh_ref[...], wrl_ref[...])


def _full(shape):
    return pl.BlockSpec(shape, lambda *_: (0,) * len(shape))


def _mixer_prompt(x, prev, mod, lw):
    nb, seq, _ = x.shape
    tile = min(MIX_TILE, seq)
    nt = seq // tile
    tok = pl.BlockSpec((1, tile, D_MODEL), lambda b, s: (b, s, 0))
    modspec = pl.BlockSpec((1, 1, N_MOD * D_MODEL), lambda b, s: (b, 0, 0))
    args, specs = [x], [tok]
    if prev is not None:
        args += [prev[0], prev[1]]
        specs += [tok, modspec]
    args += [mod, lw["norm_mix_g"], lw["w_in"], lw["conv_a_w"], lw["conv_b_w"], lw["conv_b_bias"],
             lw["ln_b_g"], lw["ln_b_b"], lw["pool_w"], lw["pool_scale"], lw["w_out"],
             lw["norm_ffn_g"], lw["wr_hi"], lw["wr_lo"]]
    specs += [modspec] + [_full(a.shape) for a in args[len(specs) + 1:]]
    out_shape = (
        jax.ShapeDtypeStruct((nb, seq, D_MODEL), F32),
        jax.ShapeDtypeStruct((nb, seq, D_MODEL), BF16),
        jax.ShapeDtypeStruct((N_EXPERTS, nb * seq), F32),
        jax.ShapeDtypeStruct((nb, CONV_A - 1, D_A), F32),
        jax.ShapeDtypeStruct((nb, CONV_B - 1, D_B), F32),
        jax.ShapeDtypeStruct((nb, POOL_BUF, D_C), F32),
    )
    out_specs = (
        tok, tok,
        pl.BlockSpec((N_EXPERTS, tile), lambda b, s: (0, b * nt + s)),
        pl.BlockSpec((1, CONV_A - 1, D_A), lambda b, s: (b, 0, 0)),
        pl.BlockSpec((1, CONV_B - 1, D_B), lambda b, s: (b, 0, 0)),
        pl.BlockSpec((1, POOL_BUF, D_C), lambda b, s: (b, 0, 0)),
    )
    return pl.pallas_call(
        functools.partial(_mixer_prompt_kernel, prev is not None, tile),
        grid=(nb, nt),
        in_specs=specs,
        out_specs=out_specs,
        out_shape=out_shape,
        scratch_shapes=[
            pltpu.VMEM((HALO_A + tile, D_A), F32),
            pltpu.VMEM((HALO_B + tile, D_B), F32),
            pltpu.VMEM((HALO_C + tile, D_C), F32),
            pltpu.VMEM((SUBLANES - 1, HALO_B - SUBLANES + tile, D_B), F32),
            pltpu.VMEM((SUBLANES - 1, HALO_C - SUBLANES + tile, D_C), F32),
        ],
        compiler_params=pltpu.CompilerParams(
            dimension_semantics=("arbitrary", "arbitrary"), vmem_limit_bytes=VMEM_LIMIT_BYTES),
        name="mixer_prompt",
    )(*args)


def _mixer_sample_kernel(has_prev, *refs):
    if has_prev:
        x_ref, f_ref, modp_ref, *refs = refs
    else:
        x_ref, *refs = refs
    (mod_ref, sa_ref, sb_ref, sc_ref, ng_ref, win_ref, caw_ref, cbw_ref, cbb_ref, lng_ref, lnb_ref,
     pw_ref, ps_ref, wout_ref, nfg_ref, wrh_ref, wrl_ref,
     xmid_ref, h2_ref, lt_ref, na_ref, nb_ref, nc_ref) = refs

    t_new, n_seq, _ = x_ref.shape
    rows = t_new * n_seq
    x = x_ref[...]
    if has_prev:
        x = x + _mod_parts(modp_ref[...])[5][None] * f_ref[...]
    sh_m, sc_m, g_m, sh_f, sc_f, g_f = [m[None] for m in _mod_parts(mod_ref[...])]

    h = _rms_mod(x, ng_ref[...], sc_m, sh_m)
    proj = _dot_ref(h.reshape(rows, D_MODEL), win_ref).reshape(t_new, n_seq, D_IN)
    xa = proj[:, :, 0:D_A]
    ca = proj[:, :, D_A:2 * D_A]
    ba = proj[:, :, 2 * D_A:3 * D_A]
    a_b = proj[:, :, 3 * D_A:3 * D_A + D_B]
    g_b = proj[:, :, 3 * D_A + D_B:3 * D_A + 2 * D_B]
    p_c = proj[:, :, 3 * D_A + 2 * D_B:]

    def history(state_ref, new):
        return [state_ref[j] for j in range(state_ref.shape[0])] + [new[t] for t in range(t_new)]

    def conv(seq, w_ref, taps):
        outs = []
        for t in range(t_new):
            acc = w_ref[0:1, :] * seq[t]
            for k in range(1, taps):
                acc = acc + w_ref[k:k + 1, :] * seq[t + k]
            outs.append(acc)
        return jnp.stack(outs)

    def emit_state(out_ref, seq):
        keep = out_ref.shape[0]
        for j in range(keep):
            out_ref[j] = seq[len(seq) - keep + j]

    seq_a = history(sa_ref, ca * xa)
    ya = ba * conv(seq_a, caw_ref, CONV_A)
    emit_state(na_ref, seq_a)

    seq_b = history(sb_ref, a_b * jax.nn.sigmoid(g_b))
    vb = conv(seq_b, cbw_ref, CONV_B) + cbb_ref[...]
    yb = _silu(_layer_norm(vb, lng_ref[...], lnb_ref[...]))
    emit_state(nb_ref, seq_b)

    seq_c = history(sc_ref, p_c)
    pooled = []
    for t in range(t_new):
        cur = POOL_BUF + t
        run = seq_c[cur]
        sums, counts = [], []
        for j in range(1, POOL_WINDOWS[-1]):
            run = run + seq_c[cur - j]
            if j + 1 in POOL_WINDOWS:
                sums.append(run)
                counts.append(float(min(PAST_LEN + t + 1, j + 1)))
        pooled.append(_pool_select(sums, counts, seq_c[cur]))
    d = jnp.stack(pooled)
    yc = _dot_ref(d.reshape(rows, D_C), pw_ref).reshape(t_new, n_seq, D_C) * ps_ref[...]
    emit_state(nc_ref, seq_c)

    mix = jnp.concatenate([ya, yb, yc], axis=-1).reshape(rows, D_MODEL)
    xmid = x + g_m * _dot_ref(mix, wout_ref).reshape(t_new, n_seq, D_MODEL)
    xmid_ref[...] = xmid

    h2 = _rms_mod(xmid, nfg_ref[...], sc_f, sh_f).reshape(rows, D_MODEL)
    h2_hi = h2.astype(BF16)
    h2_ref[...] = h2_hi
    lt_ref[...] = _router_logits_t(h2, h2_hi, wrh_ref[...], wrl_ref[...])


def _mixer_sample(x, prev, mod, states, lw):
    t_new, n_seq, _ = x.shape
    rows = t_new * n_seq
    args = [x]
    if prev is not None:
        args += [prev[0], prev[1]]
    args += [mod, *states, lw["norm_mix_g"], lw["w_in"], lw["conv_a_w"], lw["conv_b_w"],
             lw["conv_b_bias"], lw["ln_b_g"], lw["ln_b_b"], lw["pool_w"], lw["pool_scale"],
             lw["w_out"], lw["norm_ffn_g"], lw["wr_hi"], lw["wr_lo"]]
    out_shape = (
        jax.ShapeDtypeStruct((t_new, n_seq, D_MODEL), F32),
        jax.ShapeDtypeStruct((rows, D_MODEL), BF16),
        jax.ShapeDtypeStruct((N_EXPERTS, rows), F32),
        jax.ShapeDtypeStruct((CONV_A - 1, n_seq, D_A), F32),
        jax.ShapeDtypeStruct((CONV_B - 1, n_seq, D_B), F32),
        jax.ShapeDtypeStruct((POOL_BUF, n_seq, D_C), F32),
    )
    return pl.pallas_call(
        functools.partial(_mixer_sample_kernel, prev is not None),
        grid=(1,),
        in_specs=[_full(a.shape) for a in args],
        out_specs=tuple(_full(o.shape) for o in out_shape),
        out_shape=out_shape,
        compiler_params=pltpu.CompilerParams(
            dimension_semantics=("arbitrary",), vmem_limit_bytes=VMEM_LIMIT_BYTES),
        name="mixer_sample",
    )(*args)


def _route(lt, bias):
    n_tok = lt.shape[1]
    scores = jax.nn.sigmoid(lt)
    biased = scores + bias
    sj = [scores[GROUP_SIZE * j:GROUP_SIZE * (j + 1)] for j in range(GROUP_SIZE)]
    bj = [biased[GROUP_SIZE * j:GROUP_SIZE * (j + 1)] for j in range(GROUP_SIZE)]

    m1 = bj[0]
    m2 = jnp.full_like(m1, -jnp.inf)
    for j in range(1, GROUP_SIZE):
        m2 = jnp.maximum(m2, jnp.minimum(m1, bj[j]))
        m1 = jnp.maximum(m1, bj[j])
    gscore = m1 + m2

    gidx = lax.broadcasted_iota(jnp.int32, (N_EXPERT_GROUPS, n_tok), 0)
    lower = [None] + [jnp.where(gidx >= k, gidx - k, gidx - k + N_EXPERT_GROUPS) < gidx
                      for k in range(1, N_EXPERT_GROUPS)]

    def beats(other, mine, tie_wins):
        return (other > mine) | ((other == mine) & tie_wins)

    grank = jnp.zeros((N_EXPERT_GROUPS, n_tok), jnp.int32)
    for k in range(1, N_EXPERT_GROUPS):
        grank = grank + beats(pltpu.roll(gscore, k, 0), gscore, lower[k]).astype(jnp.int32)
    gsel = grank < TOPK_GROUPS

    def over_groups(op, v):
        for k in (4, 2, 1):
            v = op(v, pltpu.roll(v, k, 0))
        return v

    mj = [jnp.where(gsel, b, -jnp.inf) for b in bj]
    eid = [(gidx * GROUP_SIZE + j).astype(F32) for j in range(GROUP_SIZE)]
    sel = [jnp.zeros((N_EXPERT_GROUPS, n_tok), F32) for _ in range(GROUP_SIZE)]
    for _ in range(TOP_K):
        top = over_groups(jnp.maximum, functools.reduce(jnp.maximum, mj))
        first = over_groups(jnp.minimum, functools.reduce(
            jnp.minimum,
            [jnp.where(mj[j] == top, eid[j], float(N_EXPERTS)) for j in range(GROUP_SIZE)]))
        for j in range(GROUP_SIZE):
            hit = eid[j] == first
            sel[j] = jnp.where(hit, 1.0, sel[j])
            mj[j] = jnp.where(hit, -jnp.inf, mj[j])

    picked = [sel[j] * sj[j] for j in range(GROUP_SIZE)]
    tot = picked[0]
    for j in range(1, GROUP_SIZE):
        tot = tot + picked[j]
    denom = jnp.sum(tot, axis=0, keepdims=True)
    gates = [picked[j] / denom * ROUTED_SCALE for j in range(GROUP_SIZE)]
    return jnp.concatenate(sel, axis=0), jnp.concatenate(gates, axis=0)


def _moe_kernel(n_blocks, h2_ref, lt_ref, rb_ref, wg_ref, wu_ref, wd_ref, wsg_ref, wsu_ref,
                wsd_ref, o_ref, rank_ref, gate_ref, p_ref, xs_ref, ys_ref, gs_ref, codes_ref):
    q = pl.program_id(1)
    seg = MOE_CAP
    blk = MOE_BLOCK

    @pl.when(q == 0)
    def _():
        sel, gates = _route(lt_ref[...], rb_ref[...])
        gate_ref[...] = gates
        before = jnp.where(lax.broadcasted_iota(jnp.int32, (blk, blk), 0)
                           < lax.broadcasted_iota(jnp.int32, (blk, blk), 1), 1.0, 0.0).astype(BF16)
        passes = []
        for b in range(n_blocks):
            sb = sel[:, b * blk:(b + 1) * blk]
            cnt = jnp.dot(sb.astype(BF16), before, preferred_element_type=F32)
            rank_ref[:, b * blk:(b + 1) * blk] = jnp.where(sb > 0.5, cnt, -1.0)
            total = jnp.broadcast_to(jnp.max((cnt + 1.0) * sb, axis=1, keepdims=True),
                                     (N_EXPERTS, LANES))
            passes.append(((total + (seg - 1)) * (1.0 / seg)).astype(jnp.int32))
        row = lax.broadcasted_iota(jnp.int32, (N_EXPERT_GROUPS, LANES), 0) % MOE_SLOTS
        field = functools.reduce(lambda a, s: jnp.where(row == s, CODE_RADIX ** s, a),
                                 range(1, MOE_SLOTS), jnp.ones_like(row))

        def over_slots(op, v):
            v = op(v, pltpu.roll(v, 1, 0))
            return op(v, pltpu.roll(v, 2, 0))

        for j in range(GROUP_SIZE):
            slab = [p[N_EXPERT_GROUPS * j:N_EXPERT_GROUPS * (j + 1)] for p in passes]
            slot_code = over_slots(jnp.add, functools.reduce(jnp.maximum, slab) * field)
            blk_code = functools.reduce(
                jnp.add, [over_slots(jnp.maximum, slab[b]) * CODE_RADIX ** b for b in range(n_blocks)])
            for i in range(N_EXPERT_GROUPS // MOE_SLOTS):
                last = MOE_SLOTS * (i + 1) - 1
                step = (N_EXPERT_GROUPS // MOE_SLOTS) * j + i
                codes_ref[step] = jnp.max(slot_code[last:last + 1])
                codes_ref[N_EXPERTS // MOE_SLOTS + step] = jnp.max(blk_code[last:last + 1])
        x = h2_ref[...]
        hid = _silu(jnp.dot(x, wsg_ref[...], preferred_element_type=F32)) * jnp.dot(
            x, wsu_ref[...], preferred_element_type=F32)
        o_ref[...] = jnp.dot(hid.astype(BF16), wsd_ref[...], preferred_element_type=F32)

    r0 = q * MOE_SLOTS
    pos = lax.broadcasted_iota(jnp.int32, (seg, blk), 0).astype(F32)

    def passes_of(code, i):
        return lax.shift_right_logical(code, 3 * i) & (CODE_RADIX - 1)

    slot_passes = [passes_of(codes_ref[q], s) for s in range(MOE_SLOTS)]
    blk_passes = [passes_of(codes_ref[N_EXPERTS // MOE_SLOTS + q], b) for b in range(n_blocks)]

    def gather(b, p):
        base = (p * seg).astype(F32) if not isinstance(p, int) else float(p * seg)
        rows = []
        for s in range(MOE_SLOTS):
            rrow = rank_ref[pl.ds(r0 + s, 1), b * blk:(b + 1) * blk]
            grow = gate_ref[pl.ds(r0 + s, 1), b * blk:(b + 1) * blk]
            match = (rrow - base) == pos
            rows.append(jnp.where(match, 1.0, 0.0).astype(BF16))
            gs_ref[s, b] = jnp.sum(jnp.where(match, grow, 0.0), axis=1, keepdims=True)
        onehot = jnp.concatenate(rows, axis=0)
        p_ref[b] = onehot
        xs = jnp.dot(onehot, h2_ref[b * blk:(b + 1) * blk, :],
                     preferred_element_type=F32).astype(BF16)
        for s in range(MOE_SLOTS):
            xs_ref[s, b] = xs[s * seg:(s + 1) * seg]

    def expert_mlp(s):
        lhs = xs_ref[s].reshape(n_blocks * seg, D_MODEL)
        hid = _silu(jnp.dot(lhs, wg_ref[s, 0], preferred_element_type=F32)) * jnp.dot(
            lhs, wu_ref[s, 0], preferred_element_type=F32)
        y = jnp.dot(hid.astype(BF16), wd_ref[s, 0], preferred_element_type=F32)
        y = gs_ref[s].reshape(n_blocks * seg, 1) * y
        ys_ref[s] = y.astype(BF16).reshape(n_blocks, seg, D_MODEL)

    def combine(b):
        yb = jnp.concatenate([ys_ref[s, b] for s in range(MOE_SLOTS)], axis=0)
        o_ref[b * blk:(b + 1) * blk, :] += lax.dot_general(
            p_ref[b], yb, (((0,), (0,)), ((), ())), preferred_element_type=F32)

    for b in range(n_blocks):
        gather(b, 0)
    for s in range(MOE_SLOTS):
        expert_mlp(s)
    for b in range(n_blocks):
        combine(b)

    def overflow_pass(p, carry):
        for b in range(n_blocks):
            pl.when(blk_passes[b] > p)(functools.partial(gather, b, p))
        for s in range(MOE_SLOTS):
            pl.when(slot_passes[s] > p)(functools.partial(expert_mlp, s))
        for b in range(n_blocks):
            pl.when(blk_passes[b] > p)(functools.partial(combine, b))
        return carry

    lax.fori_loop(1, functools.reduce(jnp.maximum, slot_passes), overflow_pass, 0)


def _moe(h2, lt, lw):
    n_tok = h2.shape[0]
    sup = min(MOE_SUPER, n_tok)
    n_blocks = sup // MOE_BLOCK
    n_steps = N_EXPERTS // MOE_SLOTS
    half = N_EXPERT_GROUPS // MOE_SLOTS
    layer = lw["layer"]
    assert n_blocks * 3 <= 30, "block pass counts are packed as 3-bit fields of one int32"
    wspec = lambda a, b: pl.BlockSpec(
        (None, MOE_SLOTS, 1, a, b), lambda t, q: (layer, q % half, q // half, 0, 0))
    return pl.pallas_call(
        functools.partial(_moe_kernel, n_blocks),
        grid=(n_tok // sup, n_steps),
        in_specs=[
            pl.BlockSpec((sup, D_MODEL), lambda t, q: (t, 0)),
            pl.BlockSpec((N_EXPERTS, sup), lambda t, q: (0, t)),
            _full((N_EXPERTS, 1)),
            wspec(D_MODEL, D_EXPERT), wspec(D_MODEL, D_EXPERT), wspec(D_EXPERT, D_MODEL),
            _full(lw["ws_gate"].shape), _full(lw["ws_up"].shape), _full(lw["ws_down"].shape),
        ],
        out_specs=pl.BlockSpec((sup, D_MODEL), lambda t, q: (t, 0)),
        out_shape=jax.ShapeDtypeStruct((n_tok, D_MODEL), F32),
        scratch_shapes=[
            pltpu.VMEM((N_EXPERTS, sup), F32),
            pltpu.VMEM((N_EXPERTS, sup), F32),
            pltpu.VMEM((n_blocks, MOE_SLOTS * MOE_CAP, MOE_BLOCK), BF16),
            pltpu.VMEM((MOE_SLOTS, n_blocks, MOE_CAP, D_MODEL), BF16),
            pltpu.VMEM((MOE_SLOTS, n_blocks, MOE_CAP, D_MODEL), BF16),
            pltpu.VMEM((MOE_SLOTS, n_blocks, MOE_CAP, 1), F32),
            pltpu.SMEM((2 * n_steps,), jnp.int32),
        ],
        compiler_params=pltpu.CompilerParams(
            dimension_semantics=("arbitrary", "arbitrary"), vmem_limit_bytes=VMEM_LIMIT_BYTES),
        name="moe",
    )(h2, lt, lw["router_bias"], lw["w_gate"], lw["w_up"], lw["w_down"],
      lw["ws_gate"], lw["ws_up"], lw["ws_down"])


def _final_kernel(x_ref, f_ref, mod_ref, g_ref, o_ref):
    gate = _mod_parts(mod_ref[...])[5]
    if x_ref.ndim == 3 and mod_ref.ndim == 2:
        gate = gate[None]
    x = x_ref[...] + gate * f_ref[...]
    ms = jnp.mean(x * x, axis=-1, keepdims=True)
    o_ref[...] = (x * lax.rsqrt(ms + EPS)) * g_ref[...]


def _final_prompt(xmid, ffn, mod, g):
    nb, seq, _ = xmid.shape
    tile = min(MIX_TILE, seq)
    tok = pl.BlockSpec((1, tile, D_MODEL), lambda b, s: (b, s, 0))
    return pl.pallas_call(
        _final_kernel,
        grid=(nb, seq // tile),
        in_specs=[tok, tok, pl.BlockSpec((1, 1, N_MOD * D_MODEL), lambda b, s: (b, 0, 0)),
                  _full(g.shape)],
        out_specs=tok,
        out_shape=jax.ShapeDtypeStruct(xmid.shape, F32),
        compiler_params=pltpu.CompilerParams(
            dimension_semantics=("arbitrary", "arbitrary"), vmem_limit_bytes=VMEM_LIMIT_BYTES),
        name="final_prompt",
    )(xmid, ffn, mod, g)


def _final_sample(xmid, ffn, mod, g):
    return pl.pallas_call(
        _final_kernel,
        grid=(1,),
        in_specs=[_full(xmid.shape), _full(ffn.shape), _full(mod.shape), _full(g.shape)],
        out_specs=_full(xmid.shape),
        out_shape=jax.ShapeDtypeStruct(xmid.shape, F32),
        compiler_params=pltpu.CompilerParams(
            dimension_semantics=("arbitrary",), vmem_limit_bytes=VMEM_LIMIT_BYTES),
        name="final_sample",
    )(xmid, ffn, mod, g)


def _slot_order(a, axis):
    shape = a.shape
    a = a.reshape(shape[:axis] + (N_EXPERT_GROUPS, GROUP_SIZE) + shape[axis + 1:])
    return jnp.swapaxes(a, axis, axis + 1).reshape(shape)


def _layer_weights(l, w_in, norm_mix_g, conv_a_w, conv_b_w, conv_b_bias, ln_b_g, ln_b_b, pool_w,
                   pool_scale, w_out, norm_ffn_g, w_router, router_bias, w_gate, w_up, w_down,
                   ws_gate, ws_up, ws_down):
    row = lambda v: v[l].reshape(1, -1)
    wr = _slot_order(w_router[l], 1).T
    wr_hi = wr.astype(BF16)
    blockdiag = jax.scipy.linalg.block_diag(*[pool_w[l, g] for g in range(pool_w.shape[1])])
    precise = l + 1 < w_in.shape[0]
    stacked = lambda w: jnp.stack(_split(w)) if precise else w.astype(BF16)[None]
    return dict(
        norm_mix_g=row(norm_mix_g), w_in=stacked(w_in[l]), conv_a_w=conv_a_w[l],
        conv_b_w=conv_b_w[l], conv_b_bias=row(conv_b_bias), ln_b_g=row(ln_b_g), ln_b_b=row(ln_b_b),
        pool_w=stacked(blockdiag), pool_scale=row(pool_scale), w_out=stacked(w_out[l]),
        norm_ffn_g=row(norm_ffn_g), wr_hi=wr_hi, wr_lo=(wr - wr_hi.astype(F32)).astype(BF16),
        router_bias=_slot_order(router_bias[l], 0).reshape(N_EXPERTS, 1),
        layer=l, w_gate=w_gate, w_up=w_up, w_down=w_down,
        ws_gate=ws_gate[l].astype(BF16), ws_up=ws_up[l].astype(BF16), ws_down=ws_down[l].astype(BF16),
    )


def kernel(x_prompt, x_sample, c_prompt, c_sample, state_conv_a, state_conv_b, state_pool, w_ada, b_ada, norm_mix_g, w_in, conv_a_w, conv_b_w, conv_b_bias, ln_b_g, ln_b_b, pool_w, pool_scale, w_out, norm_ffn_g, w_router, router_bias, w_gate, w_up, w_down, ws_gate, ws_up, ws_down, final_norm_g):
    depth = w_ada.shape[0]
    n_p, seq, _ = x_prompt.shape
    n_s, t_new, _ = x_sample.shape

    mod = _ada(jnp.concatenate([c_prompt, c_sample], axis=0), w_ada, b_ada)
    mod_p = mod[:, :n_p].reshape(depth, n_p, 1, N_MOD * D_MODEL)
    mod_s = mod[:, n_p:]

    xp = x_prompt
    xs = jnp.swapaxes(x_sample, 0, 1)
    time_major = lambda st: jnp.swapaxes(st, 1, 2)
    st_a, st_b, st_c = time_major(state_conv_a), time_major(state_conv_b), time_major(state_pool)

    experts = lambda w: w.astype(BF16).reshape((depth, N_EXPERT_GROUPS, GROUP_SIZE) + w.shape[2:])
    w_gate, w_up, w_down = experts(w_gate), experts(w_up), experts(w_down)

    prev_p = prev_s = None
    new_p, new_s = [], []
    for l in range(depth):
        lw = _layer_weights(l, w_in, norm_mix_g, conv_a_w, conv_b_w, conv_b_bias, ln_b_g, ln_b_b,
                            pool_w, pool_scale, w_out, norm_ffn_g, w_router, router_bias,
                            w_gate, w_up, w_down, ws_gate, ws_up, ws_down)
        xp, h2p, ltp, na, nb, nc = _mixer_prompt(xp, prev_p, mod_p[l], lw)
        new_p.append((na, nb, nc))
        ffn_p = _moe(h2p.reshape(n_p * seq, D_MODEL), ltp, lw).reshape(n_p, seq, D_MODEL)
        prev_p = (ffn_p, mod_p[l])

        xs, h2s, lts, na, nb, nc = _mixer_sample(xs, prev_s, mod_s[l], (st_a[l], st_b[l], st_c[l]), lw)
        new_s.append((na, nb, nc))
        ffn_s = _moe(h2s, lts, lw).reshape(t_new, n_s, D_MODEL)
        prev_s = (ffn_s, mod_s[l])

    g = final_norm_g.reshape(1, D_MODEL)
    y_prompt = _final_prompt(xp, prev_p[0], prev_p[1], g)
    y_sample = jnp.swapaxes(_final_sample(xs, prev_s[0], prev_s[1], g), 0, 1)

    stack = lambda items, i: jnp.stack([it[i] for it in items])
    batch_major = lambda a: jnp.swapaxes(a, 1, 2)
    return (y_prompt, y_sample,
            stack(new_p, 0), stack(new_p, 1), stack(new_p, 2),
            batch_major(stack(new_s, 0)), batch_major(stack(new_s, 1)), batch_major(stack(new_s, 2)))
```

```python
import functools

import jax
import jax.numpy as jnp
from jax import lax
from jax.experimental import pallas as pl
from jax.experimental.pallas import tpu as pltpu

D_MODEL = 1024
HEAD_DIM = 64
D_A = 6 * HEAD_DIM
D_B = 6 * HEAD_DIM
D_C = D_MODEL - D_A - D_B
C_GROUP = 64
POOL_WINDOWS = (2, 4, 8, 16)
POOL_BUF = max(POOL_WINDOWS) - 1
CONV_A = 3
CONV_B = 31
D_IN = 3 * D_A + 2 * D_B + D_C
N_EXPERTS = 64
N_EXPERT_GROUPS = 8
GROUP_SIZE = N_EXPERTS // N_EXPERT_GROUPS
TOPK_GROUPS = 4
TOP_K = 8
D_EXPERT = 256
ROUTED_SCALE = 2.5
N_MOD = 6
EPS = 1e-6
PAST_LEN = 16384

F32 = jnp.float32
BF16 = jnp.bfloat16

VMEM_LIMIT_BYTES = 56 * 1024 * 1024

MIX_TILE = 512
HALO_A = 8
HALO_B = 32
HALO_C = 16

MOE_BLOCK = 256
MOE_CAP = 64
MOE_MAIN = 48
CODE_RADIX = 8
LANES = 128
SUBLANES = 8
MOE_SLOTS = 4
MOE_SUPER = 2048


def _silu(v):
    return v * jax.nn.sigmoid(v)


def _rms_mod(x, g, scale, shift):
    ms = jnp.mean(x * x, axis=-1, keepdims=True)
    y = (x * lax.rsqrt(ms + EPS)) * g
    return y * (1.0 + scale) + shift


def _split(v):
    hi = v.astype(BF16)
    return hi, (v - hi.astype(F32)).astype(BF16)


def _dot(a, w_hi, w_lo=None):
    a_hi, a_lo = _split(a)
    out = jnp.dot(a_hi, w_hi, preferred_element_type=F32)
    if w_lo is not None:
        out = out + jnp.dot(a_lo, w_hi, preferred_element_type=F32)
        out = out + jnp.dot(a_hi, w_lo, preferred_element_type=F32)
    return out


def _dot_ref(a, w_ref):
    return _dot(a, w_ref[0], w_ref[1] if w_ref.shape[0] == 2 else None)


def _mod_parts(mod):
    return [mod[..., i * D_MODEL:(i + 1) * D_MODEL] for i in range(N_MOD)]


def _ada_kernel(c_ref, w_ref, b_ref, o_ref):
    w_hi, w_lo = _split(w_ref[0])
    o_ref[0] = _dot(_silu(c_ref[...]), w_hi, w_lo) + b_ref[0]


def _ada(c_all, w_ada, b_ada):
    depth, _, n_out = w_ada.shape
    rows = c_all.shape[0]
    tn = 1536
    return pl.pallas_call(
        _ada_kernel,
        grid=(depth, n_out // tn),
        in_specs=[
            pl.BlockSpec((rows, D_MODEL), lambda l, n: (0, 0)),
            pl.BlockSpec((1, D_MODEL, tn), lambda l, n: (l, 0, n)),
            pl.BlockSpec((1, 1, tn), lambda l, n: (l, 0, n)),
        ],
        out_specs=pl.BlockSpec((1, rows, tn), lambda l, n: (l, 0, n)),
        out_shape=jax.ShapeDtypeStruct((depth, rows, n_out), F32),
        compiler_params=pltpu.CompilerParams(
            dimension_semantics=("arbitrary", "arbitrary"), vmem_limit_bytes=VMEM_LIMIT_BYTES),
        name="ada",
    )(c_all, w_ada, b_ada.reshape(depth, 1, n_out))


def _layer_norm(v, g, b):
    mu = jnp.mean(v, axis=-1, keepdims=True)
    d = v - mu
    var = jnp.mean(d * d, axis=-1, keepdims=True)
    return d * lax.rsqrt(var + EPS) * g + b


def _pool_select(sums, counts, p):
    lane = lax.broadcasted_iota(jnp.int32, p.shape, p.ndim - 1)
    pooled = sums[-1] / counts[-1]
    for gi in range(len(POOL_WINDOWS) - 2, -1, -1):
        pooled = jnp.where(lane < (gi + 1) * C_GROUP, sums[gi] / counts[gi], pooled)
    return pooled - p


def _router_logits_t(h2, h2_hi, wr_hi, wr_lo):
    h2_lo = (h2 - h2_hi.astype(F32)).astype(BF16)
    nt = (((1,), (1,)), ((), ()))
    both = lax.dot_general(jnp.concatenate([wr_hi, wr_lo], axis=0), h2_hi, nt,
                           preferred_element_type=F32)
    lt = both[:N_EXPERTS] + lax.dot_general(wr_hi, h2_lo, nt, preferred_element_type=F32)
    return lt + both[N_EXPERTS:]


def _mixer_prompt_kernel(has_prev, tile, *refs):
    if has_prev:
        x_ref, f_ref, modp_ref, *refs = refs
    else:
        x_ref, *refs = refs
    (mod_ref, ng_ref, win_ref, caw_ref, cbw_ref, cbb_ref, lng_ref, lnb_ref, pw_ref, ps_ref,
     wout_ref, nfg_ref, wrh_ref, wrl_ref,
     xmid_ref, h2_ref, lt_ref, na_ref, nb_ref, nc_ref,
     va_ref, ub_ref, pc_ref, ubs_ref, pcs_ref) = refs

    s = pl.program_id(1)

    @pl.when(s == 0)
    def _():
        va_ref[0:HALO_A, :] = jnp.zeros((HALO_A, D_A), F32)
        ub_ref[0:HALO_B, :] = jnp.zeros((HALO_B, D_B), F32)
        pc_ref[0:HALO_C, :] = jnp.zeros((HALO_C, D_C), F32)

    x = x_ref[0]
    if has_prev:
        x = x + _mod_parts(modp_ref[0])[5] * f_ref[0]
    sh_m, sc_m, g_m, sh_f, sc_f, g_f = _mod_parts(mod_ref[0])

    proj = _dot_ref(_rms_mod(x, ng_ref[...], sc_m, sh_m), win_ref)
    xa = proj[:, 0:D_A]
    ca = proj[:, D_A:2 * D_A]
    ba = proj[:, 2 * D_A:3 * D_A]
    a_b = proj[:, 3 * D_A:3 * D_A + D_B]
    g_b = proj[:, 3 * D_A + D_B:3 * D_A + 2 * D_B]
    p_c = proj[:, 3 * D_A + 2 * D_B:]

    va_ref[HALO_A:HALO_A + tile, :] = ca * xa
    acc = caw_ref[0:1, :] * va_ref[HALO_A - 2:HALO_A - 2 + tile, :]
    for k in range(1, CONV_A):
        acc = acc + caw_ref[k:k + 1, :] * va_ref[HALO_A - 2 + k:HALO_A - 2 + k + tile, :]
    ya = ba * acc
    na_ref[0] = va_ref[HALO_A + tile - (CONV_A - 1):HALO_A + tile, :]

    def window_reader(src_ref, shifted_ref):
        span = shifted_ref.shape[1]
        for r in range(1, SUBLANES):
            shifted_ref[r - 1] = src_ref[r:r + span, :]

        def window(row):
            r = row % SUBLANES
            if r == 0:
                return src_ref[row:row + tile, :]
            return shifted_ref[r - 1, row - r:row - r + tile, :]
        return window

    ub_ref[HALO_B:HALO_B + tile, :] = a_b * jax.nn.sigmoid(g_b)
    window = window_reader(ub_ref, ubs_ref)
    off = HALO_B - (CONV_B - 1)
    acc = cbw_ref[0:1, :] * window(off)
    for k in range(1, CONV_B):
        acc = acc + cbw_ref[k:k + 1, :] * window(off + k)
    yb = _silu(_layer_norm(acc + cbb_ref[...], lng_ref[...], lnb_ref[...]))
    nb_ref[0] = ub_ref[HALO_B + tile - (CONV_B - 1):HALO_B + tile, :]

    pc_ref[HALO_C:HALO_C + tile, :] = p_c
    window = window_reader(pc_ref, pcs_ref)
    pos = s * tile + lax.broadcasted_iota(jnp.int32, (tile, D_C), 0)
    run = p_c
    sums, counts = [], []
    for j in range(1, POOL_WINDOWS[-1]):
        run = run + window(HALO_C - j)
        if j + 1 in POOL_WINDOWS:
            sums.append(run)
            counts.append(jnp.minimum(pos + 1, j + 1).astype(F32))
    d = _pool_select(sums, counts, p_c)
    yc = _dot_ref(d, pw_ref) * ps_ref[...]
    nc_ref[0] = pc_ref[HALO_C + tile - POOL_BUF:HALO_C + tile, :]

    va_ref[0:HALO_A, :] = va_ref[tile:tile + HALO_A, :]
    ub_ref[0:HALO_B, :] = ub_ref[tile:tile + HALO_B, :]
    pc_ref[0:HALO_C, :] = pc_ref[tile:tile + HALO_C, :]

    mix = jnp.concatenate([ya, yb, yc], axis=-1)
    xmid = x + g_m * _dot_ref(mix, wout_ref)
    xmid_ref[0] = xmid

    h2 = _rms_mod(xmid, nfg_ref[...], sc_f, sh_f)
    h2_hi = h2.astype(BF16)
    h2_ref[0] = h2_hi
    lt_ref[...] = _router_logits_t(h2, h2_hi, wrh_ref[...], wrl_ref[...])


def _full(shape):
    return pl.BlockSpec(shape, lambda *_: (0,) * len(shape))


def _mixer_prompt(x, prev, mod, lw):
    nb, seq, _ = x.shape
    tile = min(MIX_TILE, seq)
    nt = seq // tile
    tok = pl.BlockSpec((1, tile, D_MODEL), lambda b, s: (b, s, 0))
    modspec = pl.BlockSpec((1, 1, N_MOD * D_MODEL), lambda b, s: (b, 0, 0))
    args, specs = [x], [tok]
    if prev is not None:
        args += [prev[0], prev[1]]
        specs += [tok, modspec]
    args += [mod, lw["norm_mix_g"], lw["w_in"], lw["conv_a_w"], lw["conv_b_w"], lw["conv_b_bias"],
             lw["ln_b_g"], lw["ln_b_b"], lw["pool_w"], lw["pool_scale"], lw["w_out"],
             lw["norm_ffn_g"], lw["wr_hi"], lw["wr_lo"]]
    specs += [modspec] + [_full(a.shape) for a in args[len(specs) + 1:]]
    out_shape = (
        jax.ShapeDtypeStruct((nb, seq, D_MODEL), F32),
        jax.ShapeDtypeStruct((nb, seq, D_MODEL), BF16),
        jax.ShapeDtypeStruct((N_EXPERTS, nb * seq), F32),
        jax.ShapeDtypeStruct((nb, CONV_A - 1, D_A), F32),
        jax.ShapeDtypeStruct((nb, CONV_B - 1, D_B), F32),
        jax.ShapeDtypeStruct((nb, POOL_BUF, D_C), F32),
    )
    out_specs = (
        tok, tok,
        pl.BlockSpec((N_EXPERTS, tile), lambda b, s: (0, b * nt + s)),
        pl.BlockSpec((1, CONV_A - 1, D_A), lambda b, s: (b, 0, 0)),
        pl.BlockSpec((1, CONV_B - 1, D_B), lambda b, s: (b, 0, 0)),
        pl.BlockSpec((1, POOL_BUF, D_C), lambda b, s: (b, 0, 0)),
    )
    return pl.pallas_call(
        functools.partial(_mixer_prompt_kernel, prev is not None, tile),
        grid=(nb, nt),
        in_specs=specs,
        out_specs=out_specs,
        out_shape=out_shape,
        scratch_shapes=[
            pltpu.VMEM((HALO_A + tile, D_A), F32),
            pltpu.VMEM((HALO_B + tile, D_B), F32),
            pltpu.VMEM((HALO_C + tile, D_C), F32),
            pltpu.VMEM((SUBLANES - 1, HALO_B - SUBLANES + tile, D_B), F32),
            pltpu.VMEM((SUBLANES - 1, HALO_C - SUBLANES + tile, D_C), F32),
        ],
        compiler_params=pltpu.CompilerParams(
            dimension_semantics=("arbitrary", "arbitrary"), vmem_limit_bytes=VMEM_LIMIT_BYTES),
        name="mixer_prompt",
    )(*args)


def _mixer_sample_kernel(has_prev, *refs):
    if has_prev:
        x_ref, f_ref, modp_ref, *refs = refs
    else:
        x_ref, *refs = refs
    (mod_ref, sa_ref, sb_ref, sc_ref, ng_ref, win_ref, caw_ref, cbw_ref, cbb_ref, lng_ref, lnb_ref,
     pw_ref, ps_ref, wout_ref, nfg_ref, wrh_ref, wrl_ref,
     xmid_ref, h2_ref, lt_ref, na_ref, nb_ref, nc_ref) = refs

    t_new, n_seq, _ = x_ref.shape
    rows = t_new * n_seq
    x = x_ref[...]
    if has_prev:
        x = x + _mod_parts(modp_ref[...])[5][None] * f_ref[...]
    sh_m, sc_m, g_m, sh_f, sc_f, g_f = [m[None] for m in _mod_parts(mod_ref[...])]

    h = _rms_mod(x, ng_ref[...], sc_m, sh_m)
    proj = _dot_ref(h.reshape(rows, D_MODEL), win_ref).reshape(t_new, n_seq, D_IN)
    xa = proj[:, :, 0:D_A]
    ca = proj[:, :, D_A:2 * D_A]
    ba = proj[:, :, 2 * D_A:3 * D_A]
    a_b = proj[:, :, 3 * D_A:3 * D_A + D_B]
    g_b = proj[:, :, 3 * D_A + D_B:3 * D_A + 2 * D_B]
    p_c = proj[:, :, 3 * D_A + 2 * D_B:]

    def history(state_ref, new):
        return [state_ref[j] for j in range(state_ref.shape[0])] + [new[t] for t in range(t_new)]

    def conv(seq, w_ref, taps):
        outs = []
        for t in range(t_new):
            acc = w_ref[0:1, :] * seq[t]
            for k in range(1, taps):
                acc = acc + w_ref[k:k + 1, :] * seq[t + k]
            outs.append(acc)
        return jnp.stack(outs)

    def emit_state(out_ref, seq):
        keep = out_ref.shape[0]
        for j in range(keep):
            out_ref[j] = seq[len(seq) - keep + j]

    seq_a = history(sa_ref, ca * xa)
    ya = ba * conv(seq_a, caw_ref, CONV_A)
    emit_state(na_ref, seq_a)

    seq_b = history(sb_ref, a_b * jax.nn.sigmoid(g_b))
    vb = conv(seq_b, cbw_ref, CONV_B) + cbb_ref[...]
    yb = _silu(_layer_norm(vb, lng_ref[...], lnb_ref[...]))
    emit_state(nb_ref, seq_b)

    seq_c = history(sc_ref, p_c)
    pooled = []
    for t in range(t_new):
        cur = POOL_BUF + t
        run = seq_c[cur]
        sums, counts = [], []
        for j in range(1, POOL_WINDOWS[-1]):
            run = run + seq_c[cur - j]
            if j + 1 in POOL_WINDOWS:
                sums.append(run)
                counts.append(float(min(PAST_LEN + t + 1, j + 1)))
        pooled.append(_pool_select(sums, counts, seq_c[cur]))
    d = jnp.stack(pooled)
    yc = _dot_ref(d.reshape(rows, D_C), pw_ref).reshape(t_new, n_seq, D_C) * ps_ref[...]
    emit_state(nc_ref, seq_c)

    mix = jnp.concatenate([ya, yb, yc], axis=-1).reshape(rows, D_MODEL)
    xmid = x + g_m * _dot_ref(mix, wout_ref).reshape(t_new, n_seq, D_MODEL)
    xmid_ref[...] = xmid

    h2 = _rms_mod(xmid, nfg_ref[...], sc_f, sh_f).reshape(rows, D_MODEL)
    h2_hi = h2.astype(BF16)
    h2_ref[...] = h2_hi
    lt_ref[...] = _router_logits_t(h2, h2_hi, wrh_ref[...], wrl_ref[...])


def _mixer_sample(x, prev, mod, states, lw):
    t_new, n_seq, _ = x.shape
    rows = t_new * n_seq
    args = [x]
    if prev is not None:
        args += [prev[0], prev[1]]
    args += [mod, *states, lw["norm_mix_g"], lw["w_in"], lw["conv_a_w"], lw["conv_b_w"],
             lw["conv_b_bias"], lw["ln_b_g"], lw["ln_b_b"], lw["pool_w"], lw["pool_scale"],
             lw["w_out"], lw["norm_ffn_g"], lw["wr_hi"], lw["wr_lo"]]
    out_shape = (
        jax.ShapeDtypeStruct((t_new, n_seq, D_MODEL), F32),
        jax.ShapeDtypeStruct((rows, D_MODEL), BF16),
        jax.ShapeDtypeStruct((N_EXPERTS, rows), F32),
        jax.ShapeDtypeStruct((CONV_A - 1, n_seq, D_A), F32),
        jax.ShapeDtypeStruct((CONV_B - 1, n_seq, D_B), F32),
        jax.ShapeDtypeStruct((POOL_BUF, n_seq, D_C), F32),
    )
    return pl.pallas_call(
        functools.partial(_mixer_sample_kernel, prev is not None),
        grid=(1,),
        in_specs=[_full(a.shape) for a in args],
        out_specs=tuple(_full(o.shape) for o in out_shape),
        out_shape=out_shape,
        compiler_params=pltpu.CompilerParams(
            dimension_semantics=("arbitrary",), vmem_limit_bytes=VMEM_LIMIT_BYTES),
        name="mixer_sample",
    )(*args)


def _route(lt, bias):
    n_tok = lt.shape[1]
    scores = jax.nn.sigmoid(lt)
    biased = scores + bias
    sj = [scores[GROUP_SIZE * j:GROUP_SIZE * (j + 1)] for j in range(GROUP_SIZE)]
    bj = [biased[GROUP_SIZE * j:GROUP_SIZE * (j + 1)] for j in range(GROUP_SIZE)]

    m1 = bj[0]
    m2 = jnp.full_like(m1, -jnp.inf)
    for j in range(1, GROUP_SIZE):
        m2 = jnp.maximum(m2, jnp.minimum(m1, bj[j]))
        m1 = jnp.maximum(m1, bj[j])
    gscore = m1 + m2

    gidx = lax.broadcasted_iota(jnp.int32, (N_EXPERT_GROUPS, n_tok), 0)
    lower = [None] + [jnp.where(gidx >= k, gidx - k, gidx - k + N_EXPERT_GROUPS) < gidx
                      for k in range(1, N_EXPERT_GROUPS)]

    def beats(other, mine, tie_wins):
        return (other > mine) | ((other == mine) & tie_wins)

    grank = jnp.zeros((N_EXPERT_GROUPS, n_tok), jnp.int32)
    for k in range(1, N_EXPERT_GROUPS):
        grank = grank + beats(pltpu.roll(gscore, k, 0), gscore, lower[k]).astype(jnp.int32)
    gsel = grank < TOPK_GROUPS

    def over_groups(op, v):
        for k in (4, 2, 1):
            v = op(v, pltpu.roll(v, k, 0))
        return v

    mj = [jnp.where(gsel, b, -jnp.inf) for b in bj]
    eid = [(gidx * GROUP_SIZE + j).astype(F32) for j in range(GROUP_SIZE)]
    sel = [jnp.zeros((N_EXPERT_GROUPS, n_tok), F32) for _ in range(GROUP_SIZE)]
    for _ in range(TOP_K):
        top = over_groups(jnp.maximum, functools.reduce(jnp.maximum, mj))
        first = over_groups(jnp.minimum, functools.reduce(
            jnp.minimum,
            [jnp.where(mj[j] == top, eid[j], float(N_EXPERTS)) for j in range(GROUP_SIZE)]))
        for j in range(GROUP_SIZE):
            hit = eid[j] == first
            sel[j] = jnp.where(hit, 1.0, sel[j])
            mj[j] = jnp.where(hit, -jnp.inf, mj[j])

    picked = [sel[j] * sj[j] for j in range(GROUP_SIZE)]
    tot = picked[0]
    for j in range(1, GROUP_SIZE):
        tot = tot + picked[j]
    denom = jnp.sum(tot, axis=0, keepdims=True)
    gates = [picked[j] / denom * ROUTED_SCALE for j in range(GROUP_SIZE)]
    return jnp.concatenate(sel, axis=0), jnp.concatenate(gates, axis=0)


def _moe_kernel(n_blocks, h2_ref, lt_ref, rb_ref, wg_ref, wu_ref, wd_ref, wsg_ref, wsu_ref,
                wsd_ref, o_ref, rank_ref, gate_ref, p_ref, xs_ref, ys_ref, gs_ref, codes_ref):
    q = pl.program_id(1)
    seg = MOE_CAP
    blk = MOE_BLOCK

    @pl.when(q == 0)
    def _():
        sel, gates = _route(lt_ref[...], rb_ref[...])
        gate_ref[...] = gates
        before = jnp.where(lax.broadcasted_iota(jnp.int32, (blk, blk), 0)
                           < lax.broadcasted_iota(jnp.int32, (blk, blk), 1), 1.0, 0.0).astype(BF16)
        passes = []
        most = jnp.zeros((N_EXPERTS, LANES), F32)
        for b in range(n_blocks):
            sb = sel[:, b * blk:(b + 1) * blk]
            cnt = jnp.dot(sb.astype(BF16), before, preferred_element_type=F32)
            rank_ref[:, b * blk:(b + 1) * blk] = jnp.where(sb > 0.5, cnt, -1.0)
            total = jnp.broadcast_to(jnp.max((cnt + 1.0) * sb, axis=1, keepdims=True),
                                     (N_EXPERTS, LANES))
            passes.append(((total + (seg - 1)) * (1.0 / seg)).astype(jnp.int32))
            most = jnp.maximum(most, total)
        long_seg = jnp.where(most > MOE_MAIN, 1, 0).astype(jnp.int32)
        row = lax.broadcasted_iota(jnp.int32, (N_EXPERT_GROUPS, LANES), 0) % MOE_SLOTS
        field = functools.reduce(lambda a, s: jnp.where(row == s, CODE_RADIX ** s, a),
                                 range(1, MOE_SLOTS), jnp.ones_like(row))

        def over_slots(op, v):
            v = op(v, pltpu.roll(v, 1, 0))
            return op(v, pltpu.roll(v, 2, 0))

        for j in range(GROUP_SIZE):
            slab = [p[N_EXPERT_GROUPS * j:N_EXPERT_GROUPS * (j + 1)] for p in passes]
            flags = long_seg[N_EXPERT_GROUPS * j:N_EXPERT_GROUPS * (j + 1)] * CODE_RADIX ** MOE_SLOTS
            slot_code = over_slots(jnp.add, (functools.reduce(jnp.maximum, slab) + flags) * field)
            blk_code = functools.reduce(
                jnp.add, [over_slots(jnp.maximum, slab[b]) * CODE_RADIX ** b for b in range(n_blocks)])
            for i in range(N_EXPERT_GROUPS // MOE_SLOTS):
                last = MOE_SLOTS * (i + 1) - 1
                step = (N_EXPERT_GROUPS // MOE_SLOTS) * j + i
                codes_ref[step] = jnp.max(slot_code[last:last + 1])
                codes_ref[N_EXPERTS // MOE_SLOTS + step] = jnp.max(blk_code[last:last + 1])
        x = h2_ref[...]
        hid = _silu(jnp.dot(x, wsg_ref[...], preferred_element_type=F32)) * jnp.dot(
            x, wsu_ref[...], preferred_element_type=F32)
        o_ref[...] = jnp.dot(hid.astype(BF16), wsd_ref[...], preferred_element_type=F32)

    r0 = q * MOE_SLOTS
    pos = lax.broadcasted_iota(jnp.int32, (seg, blk), 0).astype(F32)

    def passes_of(code, i):
        return lax.shift_right_logical(code, 3 * i) & (CODE_RADIX - 1)

    slot_passes = [passes_of(codes_ref[q], s) for s in range(MOE_SLOTS)]
    blk_passes = [passes_of(codes_ref[N_EXPERTS // MOE_SLOTS + q], b) for b in range(n_blocks)]

    def gather(b, p):
        base = (p * seg).astype(F32) if not isinstance(p, int) else float(p * seg)
        rows = []
        for s in range(MOE_SLOTS):
            rrow = rank_ref[pl.ds(r0 + s, 1), b * blk:(b + 1) * blk]
            grow = gate_ref[pl.ds(r0 + s, 1), b * blk:(b + 1) * blk]
            match = (rrow - base) == pos
            rows.append(jnp.where(match, 1.0, 0.0).astype(BF16))
            gs_ref[s, b] = jnp.sum(jnp.where(match, grow, 0.0), axis=1, keepdims=True)
        onehot = jnp.concatenate(rows, axis=0)
        p_ref[b] = onehot
        xs = jnp.dot(onehot, h2_ref[b * blk:(b + 1) * blk, :],
                     preferred_element_type=F32).astype(BF16)
        for s in range(MOE_SLOTS):
            xs_ref[s, b] = xs[s * seg:(s + 1) * seg]

    def expert_mlp(s, lo=0, hi=seg):
        n_rows = n_blocks * (hi - lo)
        lhs = xs_ref[s, :, lo:hi, :].reshape(n_rows, D_MODEL)
        hid = _silu(jnp.dot(lhs, wg_ref[s, 0], preferred_element_type=F32)) * jnp.dot(
            lhs, wu_ref[s, 0], preferred_element_type=F32)
        y = jnp.dot(hid.astype(BF16), wd_ref[s, 0], preferred_element_type=F32)
        y = gs_ref[s, :, lo:hi, :].reshape(n_rows, 1) * y
        ys_ref[s, :, lo:hi, :] = y.astype(BF16).reshape(n_blocks, hi - lo, D_MODEL)

    def combine(b):
        yb = jnp.concatenate([ys_ref[s, b] for s in range(MOE_SLOTS)], axis=0)
        o_ref[b * blk:(b + 1) * blk, :] += lax.dot_general(
            p_ref[b], yb, (((0,), (0,)), ((), ())), preferred_element_type=F32)

    @pl.when((q == 0) & (pl.program_id(0) == 0))
    def _():
        ys_ref[...] = jnp.zeros(ys_ref.shape, BF16)

    for b in range(n_blocks):
        gather(b, 0)
    for s in range(MOE_SLOTS):
        expert_mlp(s, 0, MOE_MAIN)
    for s in range(MOE_SLOTS):
        pl.when(passes_of(codes_ref[q], MOE_SLOTS + s) > 0)(
            functools.partial(expert_mlp, s, MOE_MAIN, seg))
    for b in range(n_blocks):
        combine(b)

    def overflow_pass(p, carry):
        for b in range(n_blocks):
            pl.when(blk_passes[b] > p)(functools.partial(gather, b, p))
        for s in range(MOE_SLOTS):
            pl.when(slot_passes[s] > p)(functools.partial(expert_mlp, s))
        for b in range(n_blocks):
            pl.when(blk_passes[b] > p)(functools.partial(combine, b))
        return carry

    lax.fori_loop(1, functools.reduce(jnp.maximum, slot_passes), overflow_pass, 0)


def _moe(h2, lt, lw):
    n_tok = h2.shape[0]
    sup = min(MOE_SUPER, n_tok)
    n_blocks = sup // MOE_BLOCK
    n_steps = N_EXPERTS // MOE_SLOTS
    half = N_EXPERT_GROUPS // MOE_SLOTS
    layer = lw["layer"]
    assert n_blocks * 3 <= 30, "block pass counts are packed as 3-bit fields of one int32"
    wspec = lambda a, b: pl.BlockSpec(
        (None, MOE_SLOTS, 1, a, b), lambda t, q: (layer, q % half, q // half, 0, 0))
    return pl.pallas_call(
        functools.partial(_moe_kernel, n_blocks),
        grid=(n_tok // sup, n_steps),
        in_specs=[
            pl.BlockSpec((sup, D_MODEL), lambda t, q: (t, 0)),
            pl.BlockSpec((N_EXPERTS, sup), lambda t, q: (0, t)),
            _full((N_EXPERTS, 1)),
            wspec(D_MODEL, D_EXPERT), wspec(D_MODEL, D_EXPERT), wspec(D_EXPERT, D_MODEL),
            _full(lw["ws_gate"].shape), _full(lw["ws_up"].shape), _full(lw["ws_down"].shape),
        ],
        out_specs=pl.BlockSpec((sup, D_MODEL), lambda t, q: (t, 0)),
        out_shape=jax.ShapeDtypeStruct((n_tok, D_MODEL), F32),
        scratch_shapes=[
            pltpu.VMEM((N_EXPERTS, sup), F32),
            pltpu.VMEM((N_EXPERTS, sup), F32),
            pltpu.VMEM((n_blocks, MOE_SLOTS * MOE_CAP, MOE_BLOCK), BF16),
            pltpu.VMEM((MOE_SLOTS, n_blocks, MOE_CAP, D_MODEL), BF16),
            pltpu.VMEM((MOE_SLOTS, n_blocks, MOE_CAP, D_MODEL), BF16),
            pltpu.VMEM((MOE_SLOTS, n_blocks, MOE_CAP, 1), F32),
            pltpu.SMEM((2 * n_steps,), jnp.int32),
        ],
        compiler_params=pltpu.CompilerParams(
            dimension_semantics=("arbitrary", "arbitrary"), vmem_limit_bytes=VMEM_LIMIT_BYTES),
        name="moe",
    )(h2, lt, lw["router_bias"], lw["w_gate"], lw["w_up"], lw["w_down"],
      lw["ws_gate"], lw["ws_up"], lw["ws_down"])


def _final_kernel(x_ref, f_ref, mod_ref, g_ref, o_ref):
    gate = _mod_parts(mod_ref[...])[5]
    if x_ref.ndim == 3 and mod_ref.ndim == 2:
        gate = gate[None]
    x = x_ref[...] + gate * f_ref[...]
    ms = jnp.mean(x * x, axis=-1, keepdims=True)
    o_ref[...] = (x * lax.rsqrt(ms + EPS)) * g_ref[...]


def _final_prompt(xmid, ffn, mod, g):
    nb, seq, _ = xmid.shape
    tile = min(MIX_TILE, seq)
    tok = pl.BlockSpec((1, tile, D_MODEL), lambda b, s: (b, s, 0))
    return pl.pallas_call(
        _final_kernel,
        grid=(nb, seq // tile),
        in_specs=[tok, tok, pl.BlockSpec((1, 1, N_MOD * D_MODEL), lambda b, s: (b, 0, 0)),
                  _full(g.shape)],
        out_specs=tok,
        out_shape=jax.ShapeDtypeStruct(xmid.shape, F32),
        compiler_params=pltpu.CompilerParams(
            dimension_semantics=("arbitrary", "arbitrary"), vmem_limit_bytes=VMEM_LIMIT_BYTES),
        name="final_prompt",
    )(xmid, ffn, mod, g)


def _final_sample(xmid, ffn, mod, g):
    return pl.pallas_call(
        _final_kernel,
        grid=(1,),
        in_specs=[_full(xmid.shape), _full(ffn.shape), _full(mod.shape), _full(g.shape)],
        out_specs=_full(xmid.shape),
        out_shape=jax.ShapeDtypeStruct(xmid.shape, F32),
        compiler_params=pltpu.CompilerParams(
            dimension_semantics=("arbitrary",), vmem_limit_bytes=VMEM_LIMIT_BYTES),
        name="final_sample",
    )(xmid, ffn, mod, g)


def _slot_order(a, axis):
    shape = a.shape
    a = a.reshape(shape[:axis] + (N_EXPERT_GROUPS, GROUP_SIZE) + shape[axis + 1:])
    return jnp.swapaxes(a, axis, axis + 1).reshape(shape)


def _layer_weights(l, w_in, norm_mix_g, conv_a_w, conv_b_w, conv_b_bias, ln_b_g, ln_b_b, pool_w,
                   pool_scale, w_out, norm_ffn_g, w_router, router_bias, w_gate, w_up, w_down,
                   ws_gate, ws_up, ws_down):
    row = lambda v: v[l].reshape(1, -1)
    wr = _slot_order(w_router[l], 1).T
    wr_hi = wr.astype(BF16)
    blockdiag = jax.scipy.linalg.block_diag(*[pool_w[l, g] for g in range(pool_w.shape[1])])
    precise = l + 1 < w_in.shape[0]
    stacked = lambda w: jnp.stack(_split(w)) if precise else w.astype(BF16)[None]
    return dict(
        norm_mix_g=row(norm_mix_g), w_in=stacked(w_in[l]), conv_a_w=conv_a_w[l],
        conv_b_w=conv_b_w[l], conv_b_bias=row(conv_b_bias), ln_b_g=row(ln_b_g), ln_b_b=row(ln_b_b),
        pool_w=stacked(blockdiag), pool_scale=row(pool_scale), w_out=stacked(w_out[l]),
        norm_ffn_g=row(norm_ffn_g), wr_hi=wr_hi, wr_lo=(wr - wr_hi.astype(F32)).astype(BF16),
        router_bias=_slot_order(router_bias[l], 0).reshape(N_EXPERTS, 1),
        layer=l, w_gate=w_gate, w_up=w_up, w_down=w_down,
        ws_gate=ws_gate[l].astype(BF16), ws_up=ws_up[l].astype(BF16), ws_down=ws_down[l].astype(BF16),
    )


def kernel(x_prompt, x_sample, c_prompt, c_sample, state_conv_a, state_conv_b, state_pool, w_ada, b_ada, norm_mix_g, w_in, conv_a_w, conv_b_w, conv_b_bias, ln_b_g, ln_b_b, pool_w, pool_scale, w_out, norm_ffn_g, w_router, router_bias, w_gate, w_up, w_down, ws_gate, ws_up, ws_down, final_norm_g):
    depth = w_ada.shape[0]
    n_p, seq, _ = x_prompt.shape
    n_s, t_new, _ = x_sample.shape

    mod = _ada(jnp.concatenate([c_prompt, c_sample], axis=0), w_ada, b_ada)
    mod_p = mod[:, :n_p].reshape(depth, n_p, 1, N_MOD * D_MODEL)
    mod_s = mod[:, n_p:]

    xp = x_prompt
    xs = jnp.swapaxes(x_sample, 0, 1)
    time_major = lambda st: jnp.swapaxes(st, 1, 2)
    st_a, st_b, st_c = time_major(state_conv_a), time_major(state_conv_b), time_major(state_pool)

    experts = lambda w: w.astype(BF16).reshape((depth, N_EXPERT_GROUPS, GROUP_SIZE) + w.shape[2:])
    w_gate, w_up, w_down = experts(w_gate), experts(w_up), experts(w_down)

    prev_p = prev_s = None
    new_p, new_s = [], []
    for l in range(depth):
        lw = _layer_weights(l, w_in, norm_mix_g, conv_a_w, conv_b_w, conv_b_bias, ln_b_g, ln_b_b,
                            pool_w, pool_scale, w_out, norm_ffn_g, w_router, router_bias,
                            w_gate, w_up, w_down, ws_gate, ws_up, ws_down)
        xp, h2p, ltp, na, nb, nc = _mixer_prompt(xp, prev_p, mod_p[l], lw)
        new_p.append((na, nb, nc))
        ffn_p = _moe(h2p.reshape(n_p * seq, D_MODEL), ltp, lw).reshape(n_p, seq, D_MODEL)
        prev_p = (ffn_p, mod_p[l])

        xs, h2s, lts, na, nb, nc = _mixer_sample(xs, prev_s, mod_s[l], (st_a[l], st_b[l], st_c[l]), lw)
        new_s.append((na, nb, nc))
        ffn_s = _moe(h2s, lts, lw).reshape(t_new, n_s, D_MODEL)
        prev_s = (ffn_s, mod_s[l])

    g = final_norm_g.reshape(1, D_MODEL)
    y_prompt = _final_prompt(xp, prev_p[0], prev_p[1], g)
    y_sample = jnp.swapaxes(_final_sample(xs, prev_s[0], prev_s[1], g), 0, 1)

    stack = lambda items, i: jnp.stack([it[i] for it in items])
    batch_major = lambda a: jnp.swapaxes(a, 1, 2)
    return (y_prompt, y_sample,
            stack(new_p, 0), stack(new_p, 1), stack(new_p, 2),
            batch_major(stack(new_s, 0)), batch_major(stack(new_s, 1)), batch_major(stack(new_s, 2)))
```

```python
import functools

import jax
import jax.numpy as jnp
from jax import lax
from jax.experimental import pallas as pl
from jax.experimental.pallas import tpu as pltpu

D_MODEL = 1024
HEAD_DIM = 64
D_A = 6 * HEAD_DIM
D_B = 6 * HEAD_DIM
D_C = D_MODEL - D_A - D_B
C_GROUP = 64
POOL_WINDOWS = (2, 4, 8, 16)
POOL_BUF = max(POOL_WINDOWS) - 1
CONV_A = 3
CONV_B = 31
D_IN = 3 * D_A + 2 * D_B + D_C
N_EXPERTS = 64
N_EXPERT_GROUPS = 8
GROUP_SIZE = N_EXPERTS // N_EXPERT_GROUPS
TOPK_GROUPS = 4
TOP_K = 8
D_EXPERT = 256
ROUTED_SCALE = 2.5
N_MOD = 6
EPS = 1e-6
PAST_LEN = 16384

F32 = jnp.float32
BF16 = jnp.bfloat16

VMEM_LIMIT_BYTES = 56 * 1024 * 1024

MIX_TILE = 512
HALO_A = 8
HALO_B = 32
HALO_C = 16

MOE_BLOCK = 256
MOE_CAP = 64
CODE_RADIX = 8
LANES = 128
SUBLANES = 8
MOE_SLOTS = 4
MOE_SUPER = 2048


def _silu(v):
    return v * jax.nn.sigmoid(v)


def _rms_mod(x, g, scale, shift):
    ms = jnp.mean(x * x, axis=-1, keepdims=True)
    y = (x * lax.rsqrt(ms + EPS)) * g
    return y * (1.0 + scale) + shift


def _split(v):
    hi = v.astype(BF16)
    return hi, (v - hi.astype(F32)).astype(BF16)


def _dot(a, w_hi, w_lo=None):
    a_hi, a_lo = _split(a)
    out = jnp.dot(a_hi, w_hi, preferred_element_type=F32)
    if w_lo is not None:
        out = out + jnp.dot(a_lo, w_hi, preferred_element_type=F32)
        out = out + jnp.dot(a_hi, w_lo, preferred_element_type=F32)
    return out


def _dot_ref(a, w_ref):
    a_hi, a_lo = _split(a)
    if w_ref.shape[0] == 3 * a.shape[1]:
        a_hi = jnp.concatenate([a_hi, a_lo, a_hi], axis=1)
    return jnp.dot(a_hi, w_ref[...], preferred_element_type=F32)


def _mod_parts(mod):
    return [mod[..., i * D_MODEL:(i + 1) * D_MODEL] for i in range(N_MOD)]


def _ada_kernel(c_ref, w_ref, b_ref, o_ref):
    w_hi, w_lo = _split(w_ref[0])
    o_ref[0] = _dot(_silu(c_ref[...]), w_hi, w_lo) + b_ref[0]


def _ada(c_all, w_ada, b_ada):
    depth, _, n_out = w_ada.shape
    rows = c_all.shape[0]
    tn = 1536
    return pl.pallas_call(
        _ada_kernel,
        grid=(depth, n_out // tn),
        in_specs=[
            pl.BlockSpec((rows, D_MODEL), lambda l, n: (0, 0)),
            pl.BlockSpec((1, D_MODEL, tn), lambda l, n: (l, 0, n)),
            pl.BlockSpec((1, 1, tn), lambda l, n: (l, 0, n)),
        ],
        out_specs=pl.BlockSpec((1, rows, tn), lambda l, n: (l, 0, n)),
        out_shape=jax.ShapeDtypeStruct((depth, rows, n_out), F32),
        compiler_params=pltpu.CompilerParams(
            dimension_semantics=("arbitrary", "arbitrary"), vmem_limit_bytes=VMEM_LIMIT_BYTES),
        name="ada",
    )(c_all, w_ada, b_ada.reshape(depth, 1, n_out))


def _layer_norm(v, g, b):
    mu = jnp.mean(v, axis=-1, keepdims=True)
    d = v - mu
    var = jnp.mean(d * d, axis=-1, keepdims=True)
    return d * lax.rsqrt(var + EPS) * g + b


def _pool_select(sums, counts, p):
    lane = lax.broadcasted_iota(jnp.int32, p.shape, p.ndim - 1)
    pooled = sums[-1] / counts[-1]
    for gi in range(len(POOL_WINDOWS) - 2, -1, -1):
        pooled = jnp.where(lane < (gi + 1) * C_GROUP, sums[gi] / counts[gi], pooled)
    return pooled - p


def _router_logits_t(h2, h2_hi, wr_hi, wr_lo):
    h2_lo = (h2 - h2_hi.astype(F32)).astype(BF16)
    nt = (((1,), (1,)), ((), ()))
    both = lax.dot_general(jnp.concatenate([wr_hi, wr_lo], axis=0), h2_hi, nt,
                           preferred_element_type=F32)
    lt = both[:N_EXPERTS] + lax.dot_general(wr_hi, h2_lo, nt, preferred_element_type=F32)
    return lt + both[N_EXPERTS:]


def _mixer_prompt_kernel(has_prev, tile, *refs):
    if has_prev:
        x_ref, f_ref, modp_ref, *refs = refs
    else:
        x_ref, *refs = refs
    (mod_ref, ng_ref, win_ref, caw_ref, cbw_ref, cbb_ref, lng_ref, lnb_ref, pw_ref, ps_ref,
     wout_ref, nfg_ref, wrh_ref, wrl_ref,
     xmid_ref, h2_ref, lt_ref, na_ref, nb_ref, nc_ref,
     va_ref, ub_ref, pc_ref, ubs_ref, pcs_ref) = refs

    s = pl.program_id(1)

    @pl.when(s == 0)
    def _():
        va_ref[0:HALO_A, :] = jnp.zeros((HALO_A, D_A), F32)
        ub_ref[0:HALO_B, :] = jnp.zeros((HALO_B, D_B), F32)
        pc_ref[0:HALO_C, :] = jnp.zeros((HALO_C, D_C), F32)

    x = x_ref[0]
    if has_prev:
        x = x + _mod_parts(modp_ref[0])[5] * f_ref[0]
    sh_m, sc_m, g_m, sh_f, sc_f, g_f = _mod_parts(mod_ref[0])

    proj = _dot_ref(_rms_mod(x, ng_ref[...], sc_m, sh_m), win_ref)
    xa = proj[:, 0:D_A]
    ca = proj[:, D_A:2 * D_A]
    ba = proj[:, 2 * D_A:3 * D_A]
    a_b = proj[:, 3 * D_A:3 * D_A + D_B]
    g_b = proj[:, 3 * D_A + D_B:3 * D_A + 2 * D_B]
    p_c = proj[:, 3 * D_A + 2 * D_B:]

    va_ref[HALO_A:HALO_A + tile, :] = ca * xa
    acc = caw_ref[0:1, :] * va_ref[HALO_A - 2:HALO_A - 2 + tile, :]
    for k in range(1, CONV_A):
        acc = acc + caw_ref[k:k + 1, :] * va_ref[HALO_A - 2 + k:HALO_A - 2 + k + tile, :]
    ya = ba * acc
    na_ref[0] = va_ref[HALO_A + tile - (CONV_A - 1):HALO_A + tile, :]

    def window_reader(src_ref, shifted_ref):
        span = shifted_ref.shape[1]
        for r in range(1, SUBLANES):
            shifted_ref[r - 1] = src_ref[r:r + span, :]

        def window(row):
            r = row % SUBLANES
            if r == 0:
                return src_ref[row:row + tile, :]
            return shifted_ref[r - 1, row - r:row - r + tile, :]
        return window

    ub_ref[HALO_B:HALO_B + tile, :] = a_b * jax.nn.sigmoid(g_b)
    window = window_reader(ub_ref, ubs_ref)
    off = HALO_B - (CONV_B - 1)
    acc = cbw_ref[0:1, :] * window(off)
    for k in range(1, CONV_B):
        acc = acc + cbw_ref[k:k + 1, :] * window(off + k)
    yb = _silu(_layer_norm(acc + cbb_ref[...], lng_ref[...], lnb_ref[...]))
    nb_ref[0] = ub_ref[HALO_B + tile - (CONV_B - 1):HALO_B + tile, :]

    pc_ref[HALO_C:HALO_C + tile, :] = p_c
    window = window_reader(pc_ref, pcs_ref)
    pos = s * tile + lax.broadcasted_iota(jnp.int32, (tile, D_C), 0)
    run = p_c
    sums, counts = [], []
    for j in range(1, POOL_WINDOWS[-1]):
        run = run + window(HALO_C - j)
        if j + 1 in POOL_WINDOWS:
            sums.append(run)
            counts.append(jnp.minimum(pos + 1, j + 1).astype(F32))
    d = _pool_select(sums, counts, p_c)
    yc = _dot_ref(d, pw_ref) * ps_ref[...]
    nc_ref[0] = pc_ref[HALO_C + tile - POOL_BUF:HALO_C + tile, :]

    va_ref[0:HALO_A, :] = va_ref[tile:tile + HALO_A, :]
    ub_ref[0:HALO_B, :] = ub_ref[tile:tile + HALO_B, :]
    pc_ref[0:HALO_C, :] = pc_ref[tile:tile + HALO_C, :]

    mix = jnp.concatenate([ya, yb, yc], axis=-1)
    xmid = x + g_m * _dot_ref(mix, wout_ref)
    xmid_ref[0] = xmid

    h2 = _rms_mod(xmid, nfg_ref[...], sc_f, sh_f)
    h2_hi = h2.astype(BF16)
    h2_ref[0] = h2_hi
    lt_ref[...] = _router_logits_t(h2, h2_hi, wrh_ref[...], wrl_ref[...])


def _full(shape):
    return pl.BlockSpec(shape, lambda *_: (0,) * len(shape))


def _resident(shape):
    return pl.BlockSpec(shape, lambda *_: (0,) * len(shape), pipeline_mode=pl.Buffered(1))


def _mixer_prompt(x, prev, mod, lw):
    nb, seq, _ = x.shape
    tile = min(MIX_TILE, seq)
    nt = seq // tile
    tok = pl.BlockSpec((1, tile, D_MODEL), lambda b, s: (b, s, 0))
    modspec = pl.BlockSpec((1, 1, N_MOD * D_MODEL), lambda b, s: (b, 0, 0))
    args, specs = [x], [tok]
    if prev is not None:
        args += [prev[0], prev[1]]
        specs += [tok, modspec]
    args += [mod, lw["norm_mix_g"], lw["w_in"], lw["conv_a_w"], lw["conv_b_w"], lw["conv_b_bias"],
             lw["ln_b_g"], lw["ln_b_b"], lw["pool_w"], lw["pool_scale"], lw["w_out"],
             lw["norm_ffn_g"], lw["wr_hi"], lw["wr_lo"]]
    specs += [modspec] + [_resident(a.shape) for a in args[len(specs) + 1:]]
    out_shape = (
        jax.ShapeDtypeStruct((nb, seq, D_MODEL), F32),
        jax.ShapeDtypeStruct((nb, seq, D_MODEL), BF16),
        jax.ShapeDtypeStruct((N_EXPERTS, nb * seq), F32),
        jax.ShapeDtypeStruct((nb, CONV_A - 1, D_A), F32),
        jax.ShapeDtypeStruct((nb, CONV_B - 1, D_B), F32),
        jax.ShapeDtypeStruct((nb, POOL_BUF, D_C), F32),
    )
    out_specs = (
        tok, tok,
        pl.BlockSpec((N_EXPERTS, tile), lambda b, s: (0, b * nt + s)),
        pl.BlockSpec((1, CONV_A - 1, D_A), lambda b, s: (b, 0, 0)),
        pl.BlockSpec((1, CONV_B - 1, D_B), lambda b, s: (b, 0, 0)),
        pl.BlockSpec((1, POOL_BUF, D_C), lambda b, s: (b, 0, 0)),
    )
    return pl.pallas_call(
        functools.partial(_mixer_prompt_kernel, prev is not None, tile),
        grid=(nb, nt),
        in_specs=specs,
        out_specs=out_specs,
        out_shape=out_shape,
        scratch_shapes=[
            pltpu.VMEM((HALO_A + tile, D_A), F32),
            pltpu.VMEM((HALO_B + tile, D_B), F32),
            pltpu.VMEM((HALO_C + tile, D_C), F32),
            pltpu.VMEM((SUBLANES - 1, HALO_B - SUBLANES + tile, D_B), F32),
            pltpu.VMEM((SUBLANES - 1, HALO_C - SUBLANES + tile, D_C), F32),
        ],
        compiler_params=pltpu.CompilerParams(
            dimension_semantics=("arbitrary", "arbitrary"), vmem_limit_bytes=VMEM_LIMIT_BYTES),
        name="mixer_prompt",
    )(*args)


def _mixer_sample_kernel(has_prev, *refs):
    if has_prev:
        x_ref, f_ref, modp_ref, *refs = refs
    else:
        x_ref, *refs = refs
    (mod_ref, sa_ref, sb_ref, sc_ref, ng_ref, win_ref, caw_ref, cbw_ref, cbb_ref, lng_ref, lnb_ref,
     pw_ref, ps_ref, wout_ref, nfg_ref, wrh_ref, wrl_ref,
     xmid_ref, h2_ref, lt_ref, na_ref, nb_ref, nc_ref) = refs

    t_new, n_seq, _ = x_ref.shape
    rows = t_new * n_seq
    x = x_ref[...]
    if has_prev:
        x = x + _mod_parts(modp_ref[...])[5][None] * f_ref[...]
    sh_m, sc_m, g_m, sh_f, sc_f, g_f = [m[None] for m in _mod_parts(mod_ref[...])]

    h = _rms_mod(x, ng_ref[...], sc_m, sh_m)
    proj = _dot_ref(h.reshape(rows, D_MODEL), win_ref).reshape(t_new, n_seq, D_IN)
    xa = proj[:, :, 0:D_A]
    ca = proj[:, :, D_A:2 * D_A]
    ba = proj[:, :, 2 * D_A:3 * D_A]
    a_b = proj[:, :, 3 * D_A:3 * D_A + D_B]
    g_b = proj[:, :, 3 * D_A + D_B:3 * D_A + 2 * D_B]
    p_c = proj[:, :, 3 * D_A + 2 * D_B:]

    def history(state_ref, new):
        return [state_ref[j] for j in range(state_ref.shape[0])] + [new[t] for t in range(t_new)]

    def conv(seq, w_ref, taps):
        outs = []
        for t in range(t_new):
            acc = w_ref[0:1, :] * seq[t]
            for k in range(1, taps):
                acc = acc + w_ref[k:k + 1, :] * seq[t + k]
            outs.append(acc)
        return jnp.stack(outs)

    def emit_state(out_ref, seq):
        keep = out_ref.shape[0]
        for j in range(keep):
            out_ref[j] = seq[len(seq) - keep + j]

    seq_a = history(sa_ref, ca * xa)
    ya = ba * conv(seq_a, caw_ref, CONV_A)
    emit_state(na_ref, seq_a)

    seq_b = history(sb_ref, a_b * jax.nn.sigmoid(g_b))
    vb = conv(seq_b, cbw_ref, CONV_B) + cbb_ref[...]
    yb = _silu(_layer_norm(vb, lng_ref[...], lnb_ref[...]))
    emit_state(nb_ref, seq_b)

    seq_c = history(sc_ref, p_c)
    pooled = []
    for t in range(t_new):
        cur = POOL_BUF + t
        run = seq_c[cur]
        sums, counts = [], []
        for j in range(1, POOL_WINDOWS[-1]):
            run = run + seq_c[cur - j]
            if j + 1 in POOL_WINDOWS:
                sums.append(run)
                counts.append(float(min(PAST_LEN + t + 1, j + 1)))
        pooled.append(_pool_select(sums, counts, seq_c[cur]))
    d = jnp.stack(pooled)
    yc = _dot_ref(d.reshape(rows, D_C), pw_ref).reshape(t_new, n_seq, D_C) * ps_ref[...]
    emit_state(nc_ref, seq_c)

    mix = jnp.concatenate([ya, yb, yc], axis=-1).reshape(rows, D_MODEL)
    xmid = x + g_m * _dot_ref(mix, wout_ref).reshape(t_new, n_seq, D_MODEL)
    xmid_ref[...] = xmid

    h2 = _rms_mod(xmid, nfg_ref[...], sc_f, sh_f).reshape(rows, D_MODEL)
    h2_hi = h2.astype(BF16)
    h2_ref[...] = h2_hi
    lt_ref[...] = _router_logits_t(h2, h2_hi, wrh_ref[...], wrl_ref[...])


def _mixer_sample(x, prev, mod, states, lw):
    t_new, n_seq, _ = x.shape
    rows = t_new * n_seq
    args = [x]
    if prev is not None:
        args += [prev[0], prev[1]]
    args += [mod, *states, lw["norm_mix_g"], lw["w_in"], lw["conv_a_w"], lw["conv_b_w"],
             lw["conv_b_bias"], lw["ln_b_g"], lw["ln_b_b"], lw["pool_w"], lw["pool_scale"],
             lw["w_out"], lw["norm_ffn_g"], lw["wr_hi"], lw["wr_lo"]]
    out_shape = (
        jax.ShapeDtypeStruct((t_new, n_seq, D_MODEL), F32),
        jax.ShapeDtypeStruct((rows, D_MODEL), BF16),
        jax.ShapeDtypeStruct((N_EXPERTS, rows), F32),
        jax.ShapeDtypeStruct((CONV_A - 1, n_seq, D_A), F32),
        jax.ShapeDtypeStruct((CONV_B - 1, n_seq, D_B), F32),
        jax.ShapeDtypeStruct((POOL_BUF, n_seq, D_C), F32),
    )
    return pl.pallas_call(
        functools.partial(_mixer_sample_kernel, prev is not None),
        grid=(1,),
        in_specs=[_full(a.shape) for a in args],
        out_specs=tuple(_full(o.shape) for o in out_shape),
        out_shape=out_shape,
        compiler_params=pltpu.CompilerParams(
            dimension_semantics=("arbitrary",), vmem_limit_bytes=VMEM_LIMIT_BYTES),
        name="mixer_sample",
    )(*args)


def _route(lt, bias):
    n_tok = lt.shape[1]
    scores = jax.nn.sigmoid(lt)
    biased = scores + bias
    sj = [scores[GROUP_SIZE * j:GROUP_SIZE * (j + 1)] for j in range(GROUP_SIZE)]
    bj = [biased[GROUP_SIZE * j:GROUP_SIZE * (j + 1)] for j in range(GROUP_SIZE)]

    m1 = bj[0]
    m2 = jnp.full_like(m1, -jnp.inf)
    for j in range(1, GROUP_SIZE):
        m2 = jnp.maximum(m2, jnp.minimum(m1, bj[j]))
        m1 = jnp.maximum(m1, bj[j])
    gscore = m1 + m2

    gidx = lax.broadcasted_iota(jnp.int32, (N_EXPERT_GROUPS, n_tok), 0)
    lower = [None] + [jnp.where(gidx >= k, gidx - k, gidx - k + N_EXPERT_GROUPS) < gidx
                      for k in range(1, N_EXPERT_GROUPS)]

    def beats(other, mine, tie_wins):
        return (other > mine) | ((other == mine) & tie_wins)

    grank = jnp.zeros((N_EXPERT_GROUPS, n_tok), jnp.int32)
    for k in range(1, N_EXPERT_GROUPS):
        grank = grank + beats(pltpu.roll(gscore, k, 0), gscore, lower[k]).astype(jnp.int32)
    gsel = grank < TOPK_GROUPS

    def over_groups(op, v):
        for k in (4, 2, 1):
            v = op(v, pltpu.roll(v, k, 0))
        return v

    mj = [jnp.where(gsel, b, -jnp.inf) for b in bj]
    eid = [(gidx * GROUP_SIZE + j).astype(F32) for j in range(GROUP_SIZE)]
    sel = [jnp.zeros((N_EXPERT_GROUPS, n_tok), F32) for _ in range(GROUP_SIZE)]
    for _ in range(TOP_K):
        top = over_groups(jnp.maximum, functools.reduce(jnp.maximum, mj))
        first = over_groups(jnp.minimum, functools.reduce(
            jnp.minimum,
            [jnp.where(mj[j] == top, eid[j], float(N_EXPERTS)) for j in range(GROUP_SIZE)]))
        for j in range(GROUP_SIZE):
            hit = eid[j] == first
            sel[j] = jnp.where(hit, 1.0, sel[j])
            mj[j] = jnp.where(hit, -jnp.inf, mj[j])

    picked = [sel[j] * sj[j] for j in range(GROUP_SIZE)]
    tot = picked[0]
    for j in range(1, GROUP_SIZE):
        tot = tot + picked[j]
    denom = jnp.sum(tot, axis=0, keepdims=True)
    gates = [picked[j] / denom * ROUTED_SCALE for j in range(GROUP_SIZE)]
    return jnp.concatenate(sel, axis=0), jnp.concatenate(gates, axis=0)


def _moe_kernel(n_blocks, h2_ref, lt_ref, rb_ref, wg_ref, wu_ref, wd_ref, wsg_ref, wsu_ref,
                wsd_ref, o_ref, rank_ref, gate_ref, p_ref, xs_ref, ys_ref, gs_ref, codes_ref):
    q = pl.program_id(1)
    seg = MOE_CAP
    blk = MOE_BLOCK

    @pl.when(q == 0)
    def _():
        sel, gates = _route(lt_ref[...], rb_ref[...])
        gate_ref[...] = gates
        before = jnp.where(lax.broadcasted_iota(jnp.int32, (blk, blk), 0)
                           < lax.broadcasted_iota(jnp.int32, (blk, blk), 1), 1.0, 0.0).astype(BF16)
        passes = []
        for b in range(n_blocks):
            sb = sel[:, b * blk:(b + 1) * blk]
            cnt = jnp.dot(sb.astype(BF16), before, preferred_element_type=F32)
            rank_ref[:, b * blk:(b + 1) * blk] = jnp.where(sb > 0.5, cnt, -1.0)
            total = jnp.broadcast_to(jnp.max((cnt + 1.0) * sb, axis=1, keepdims=True),
                                     (N_EXPERTS, LANES))
            passes.append(((total + (seg - 1)) * (1.0 / seg)).astype(jnp.int32))
        row = lax.broadcasted_iota(jnp.int32, (N_EXPERT_GROUPS, LANES), 0) % MOE_SLOTS
        field = functools.reduce(lambda a, s: jnp.where(row == s, CODE_RADIX ** s, a),
                                 range(1, MOE_SLOTS), jnp.ones_like(row))

        def over_slots(op, v):
            v = op(v, pltpu.roll(v, 1, 0))
            return op(v, pltpu.roll(v, 2, 0))

        for j in range(GROUP_SIZE):
            slab = [p[N_EXPERT_GROUPS * j:N_EXPERT_GROUPS * (j + 1)] for p in passes]
            slot_code = over_slots(jnp.add, functools.reduce(jnp.maximum, slab) * field)
            blk_code = functools.reduce(
                jnp.add, [over_slots(jnp.maximum, slab[b]) * CODE_RADIX ** b for b in range(n_blocks)])
            for i in range(N_EXPERT_GROUPS // MOE_SLOTS):
                last = MOE_SLOTS * (i + 1) - 1
                step = (N_EXPERT_GROUPS // MOE_SLOTS) * j + i
                codes_ref[step] = jnp.max(slot_code[last:last + 1])
                codes_ref[N_EXPERTS // MOE_SLOTS + step] = jnp.max(blk_code[last:last + 1])
        x = h2_ref[...]
        hid = _silu(jnp.dot(x, wsg_ref[...], preferred_element_type=F32)) * jnp.dot(
            x, wsu_ref[...], preferred_element_type=F32)
        o_ref[...] = jnp.dot(hid.astype(BF16), wsd_ref[...], preferred_element_type=F32)

    r0 = q * MOE_SLOTS
    pos = lax.broadcasted_iota(jnp.int32, (seg, blk), 0).astype(F32)

    def passes_of(code, i):
        return lax.shift_right_logical(code, 3 * i) & (CODE_RADIX - 1)

    slot_passes = [passes_of(codes_ref[q], s) for s in range(MOE_SLOTS)]
    blk_passes = [passes_of(codes_ref[N_EXPERTS // MOE_SLOTS + q], b) for b in range(n_blocks)]

    def gather(b, p):
        base = (p * seg).astype(F32) if not isinstance(p, int) else float(p * seg)
        rows = []
        for s in range(MOE_SLOTS):
            rrow = rank_ref[pl.ds(r0 + s, 1), b * blk:(b + 1) * blk]
            grow = gate_ref[pl.ds(r0 + s, 1), b * blk:(b + 1) * blk]
            match = (rrow - base) == pos
            rows.append(jnp.where(match, 1.0, 0.0).astype(BF16))
            gs_ref[s, b] = jnp.sum(jnp.where(match, grow, 0.0), axis=1, keepdims=True)
        onehot = jnp.concatenate(rows, axis=0)
        p_ref[b] = onehot
        xs = jnp.dot(onehot, h2_ref[b * blk:(b + 1) * blk, :],
                     preferred_element_type=F32).astype(BF16)
        for s in range(MOE_SLOTS):
            xs_ref[s, b] = xs[s * seg:(s + 1) * seg]

    def expert_mlp(s):
        lhs = xs_ref[s].reshape(n_blocks * seg, D_MODEL)
        hid = _silu(jnp.dot(lhs, wg_ref[s, 0], preferred_element_type=F32)) * jnp.dot(
            lhs, wu_ref[s, 0], preferred_element_type=F32)
        y = jnp.dot(hid.astype(BF16), wd_ref[s, 0], preferred_element_type=F32)
        y = gs_ref[s].reshape(n_blocks * seg, 1) * y
        ys_ref[s] = y.astype(BF16).reshape(n_blocks, seg, D_MODEL)

    def combine(b):
        yb = jnp.concatenate([ys_ref[s, b] for s in range(MOE_SLOTS)], axis=0)
        o_ref[b * blk:(b + 1) * blk, :] += lax.dot_general(
            p_ref[b], yb, (((0,), (0,)), ((), ())), preferred_element_type=F32)

    for b in range(n_blocks):
        gather(b, 0)
    for s in range(MOE_SLOTS):
        expert_mlp(s)
    for b in range(n_blocks):
        combine(b)

    def overflow_pass(p, carry):
        for b in range(n_blocks):
            pl.when(blk_passes[b] > p)(functools.partial(gather, b, p))
        for s in range(MOE_SLOTS):
            pl.when(slot_passes[s] > p)(functools.partial(expert_mlp, s))
        for b in range(n_blocks):
            pl.when(blk_passes[b] > p)(functools.partial(combine, b))
        return carry

    lax.fori_loop(1, functools.reduce(jnp.maximum, slot_passes), overflow_pass, 0)


def _moe(h2, lt, lw):
    n_tok = h2.shape[0]
    sup = min(MOE_SUPER, n_tok)
    n_blocks = sup // MOE_BLOCK
    n_steps = N_EXPERTS // MOE_SLOTS
    half = N_EXPERT_GROUPS // MOE_SLOTS
    layer = lw["layer"]
    assert n_blocks * 3 <= 30, "block pass counts are packed as 3-bit fields of one int32"
    wspec = lambda a, b: pl.BlockSpec(
        (None, MOE_SLOTS, 1, a, b), lambda t, q: (layer, q % half, q // half, 0, 0))
    return pl.pallas_call(
        functools.partial(_moe_kernel, n_blocks),
        grid=(n_tok // sup, n_steps),
        in_specs=[
            pl.BlockSpec((sup, D_MODEL), lambda t, q: (t, 0)),
            pl.BlockSpec((N_EXPERTS, sup), lambda t, q: (0, t)),
            _full((N_EXPERTS, 1)),
            wspec(D_MODEL, D_EXPERT), wspec(D_MODEL, D_EXPERT), wspec(D_EXPERT, D_MODEL),
            _full(lw["ws_gate"].shape), _full(lw["ws_up"].shape), _full(lw["ws_down"].shape),
        ],
        out_specs=pl.BlockSpec((sup, D_MODEL), lambda t, q: (t, 0)),
        out_shape=jax.ShapeDtypeStruct((n_tok, D_MODEL), F32),
        scratch_shapes=[
            pltpu.VMEM((N_EXPERTS, sup), F32),
            pltpu.VMEM((N_EXPERTS, sup), F32),
            pltpu.VMEM((n_blocks, MOE_SLOTS * MOE_CAP, MOE_BLOCK), BF16),
            pltpu.VMEM((MOE_SLOTS, n_blocks, MOE_CAP, D_MODEL), BF16),
            pltpu.VMEM((MOE_SLOTS, n_blocks, MOE_CAP, D_MODEL), BF16),
            pltpu.VMEM((MOE_SLOTS, n_blocks, MOE_CAP, 1), F32),
            pltpu.SMEM((2 * n_steps,), jnp.int32),
        ],
        compiler_params=pltpu.CompilerParams(
            dimension_semantics=("arbitrary", "arbitrary"), vmem_limit_bytes=VMEM_LIMIT_BYTES),
        name="moe",
    )(h2, lt, lw["router_bias"], lw["w_gate"], lw["w_up"], lw["w_down"],
      lw["ws_gate"], lw["ws_up"], lw["ws_down"])


def _final_kernel(x_ref, f_ref, mod_ref, g_ref, o_ref):
    gate = _mod_parts(mod_ref[...])[5]
    if x_ref.ndim == 3 and mod_ref.ndim == 2:
        gate = gate[None]
    x = x_ref[...] + gate * f_ref[...]
    ms = jnp.mean(x * x, axis=-1, keepdims=True)
    o_ref[...] = (x * lax.rsqrt(ms + EPS)) * g_ref[...]


def _final_prompt(xmid, ffn, mod, g):
    nb, seq, _ = xmid.shape
    tile = min(MIX_TILE, seq)
    tok = pl.BlockSpec((1, tile, D_MODEL), lambda b, s: (b, s, 0))
    return pl.pallas_call(
        _final_kernel,
        grid=(nb, seq // tile),
        in_specs=[tok, tok, pl.BlockSpec((1, 1, N_MOD * D_MODEL), lambda b, s: (b, 0, 0)),
                  _full(g.shape)],
        out_specs=tok,
        out_shape=jax.ShapeDtypeStruct(xmid.shape, F32),
        compiler_params=pltpu.CompilerParams(
            dimension_semantics=("arbitrary", "arbitrary"), vmem_limit_bytes=VMEM_LIMIT_BYTES),
        name="final_prompt",
    )(xmid, ffn, mod, g)


def _final_sample(xmid, ffn, mod, g):
    return pl.pallas_call(
        _final_kernel,
        grid=(1,),
        in_specs=[_full(xmid.shape), _full(ffn.shape), _full(mod.shape), _full(g.shape)],
        out_specs=_full(xmid.shape),
        out_shape=jax.ShapeDtypeStruct(xmid.shape, F32),
        compiler_params=pltpu.CompilerParams(
            dimension_semantics=("arbitrary",), vmem_limit_bytes=VMEM_LIMIT_BYTES),
        name="final_sample",
    )(xmid, ffn, mod, g)


def _slot_order(a, axis):
    shape = a.shape
    a = a.reshape(shape[:axis] + (N_EXPERT_GROUPS, GROUP_SIZE) + shape[axis + 1:])
    return jnp.swapaxes(a, axis, axis + 1).reshape(shape)


def _layer_weights(l, w_in, norm_mix_g, conv_a_w, conv_b_w, conv_b_bias, ln_b_g, ln_b_b, pool_w,
                   pool_scale, w_out, norm_ffn_g, w_router, router_bias, w_gate, w_up, w_down,
                   ws_gate, ws_up, ws_down):
    row = lambda v: v[l].reshape(1, -1)
    wr = _slot_order(w_router[l], 1).T
    wr_hi = wr.astype(BF16)
    blockdiag = jax.scipy.linalg.block_diag(*[pool_w[l, g] for g in range(pool_w.shape[1])])
    precise = l + 1 < w_in.shape[0]
    def stacked(w):
        hi, lo = _split(w)
        return jnp.concatenate([hi, hi, lo], axis=0) if precise else hi
    return dict(
        norm_mix_g=row(norm_mix_g), w_in=stacked(w_in[l]), conv_a_w=conv_a_w[l],
        conv_b_w=conv_b_w[l], conv_b_bias=row(conv_b_bias), ln_b_g=row(ln_b_g), ln_b_b=row(ln_b_b),
        pool_w=stacked(blockdiag), pool_scale=row(pool_scale), w_out=stacked(w_out[l]),
        norm_ffn_g=row(norm_ffn_g), wr_hi=wr_hi, wr_lo=(wr - wr_hi.astype(F32)).astype(BF16),
        router_bias=_slot_order(router_bias[l], 0).reshape(N_EXPERTS, 1),
        layer=l, w_gate=w_gate, w_up=w_up, w_down=w_down,
        ws_gate=ws_gate[l].astype(BF16), ws_up=ws_up[l].astype(BF16), ws_down=ws_down[l].astype(BF16),
    )


def kernel(x_prompt, x_sample, c_prompt, c_sample, state_conv_a, state_conv_b, state_pool, w_ada, b_ada, norm_mix_g, w_in, conv_a_w, conv_b_w, conv_b_bias, ln_b_g, ln_b_b, pool_w, pool_scale, w_out, norm_ffn_g, w_router, router_bias, w_gate, w_up, w_down, ws_gate, ws_up, ws_down, final_norm_g):
    depth = w_ada.shape[0]
    n_p, seq, _ = x_prompt.shape
    n_s, t_new, _ = x_sample.shape

    mod = _ada(jnp.concatenate([c_prompt, c_sample], axis=0), w_ada, b_ada)
    mod_p = mod[:, :n_p].reshape(depth, n_p, 1, N_MOD * D_MODEL)
    mod_s = mod[:, n_p:]

    xp = x_prompt
    xs = jnp.swapaxes(x_sample, 0, 1)
    time_major = lambda st: jnp.swapaxes(st, 1, 2)
    st_a, st_b, st_c = time_major(state_conv_a), time_major(state_conv_b), time_major(state_pool)

    experts = lambda w: w.astype(BF16).reshape((depth, N_EXPERT_GROUPS, GROUP_SIZE) + w.shape[2:])
    w_gate, w_up, w_down = experts(w_gate), experts(w_up), experts(w_down)

    prev_p = prev_s = None
    new_p, new_s = [], []
    for l in range(depth):
        lw = _layer_weights(l, w_in, norm_mix_g, conv_a_w, conv_b_w, conv_b_bias, ln_b_g, ln_b_b,
                            pool_w, pool_scale, w_out, norm_ffn_g, w_router, router_bias,
                            w_gate, w_up, w_down, ws_gate, ws_up, ws_down)
        xp, h2p, ltp, na, nb, nc = _mixer_prompt(xp, prev_p, mod_p[l], lw)
        new_p.append((na, nb, nc))
        ffn_p = _moe(h2p.reshape(n_p * seq, D_MODEL), ltp, lw).reshape(n_p, seq, D_MODEL)
        prev_p = (ffn_p, mod_p[l])

        xs, h2s, lts, na, nb, nc = _mixer_sample(xs, prev_s, mod_s[l], (st_a[l], st_b[l], st_c[l]), lw)
        new_s.append((na, nb, nc))
        ffn_s = _moe(h2s, lts, lw).reshape(t_new, n_s, D_MODEL)
        prev_s = (ffn_s, mod_s[l])

    g = final_norm_g.reshape(1, D_MODEL)
    y_prompt = _final_prompt(xp, prev_p[0], prev_p[1], g)
    y_sample = jnp.swapaxes(_final_sample(xs, prev_s[0], prev_s[1], g), 0, 1)

    stack = lambda items, i: jnp.stack([it[i] for it in items])
    batch_major = lambda a: jnp.swapaxes(a, 1, 2)
    return (y_prompt, y_sample,
            stack(new_p, 0), stack(new_p, 1), stack(new_p, 2),
            batch_major(stack(new_s, 0)), batch_major(stack(new_s, 1)), batch_major(stack(new_s, 2)))
```

```python
import functools

import jax
import jax.numpy as jnp
from jax import lax
from jax.experimental import pallas as pl
from jax.experimental.pallas import tpu as pltpu

D_MODEL = 1024
HEAD_DIM = 64
D_A = 6 * HEAD_DIM
D_B = 6 * HEAD_DIM
D_C = D_MODEL - D_A - D_B
C_GROUP = 64
POOL_WINDOWS = (2, 4, 8, 16)
POOL_BUF = max(POOL_WINDOWS) - 1
CONV_A = 3
CONV_B = 31
D_IN = 3 * D_A + 2 * D_B + D_C
N_EXPERTS = 64
N_EXPERT_GROUPS = 8
GROUP_SIZE = N_EXPERTS // N_EXPERT_GROUPS
TOPK_GROUPS = 4
TOP_K = 8
D_EXPERT = 256
ROUTED_SCALE = 2.5
N_MOD = 6
EPS = 1e-6
PAST_LEN = 16384

F32 = jnp.float32
BF16 = jnp.bfloat16

VMEM_LIMIT_BYTES = 56 * 1024 * 1024

MIX_TILE = 512
HALO_A = 8
HALO_B = 32
HALO_C = 16

MOE_BLOCK = 256
MOE_CAP = 64
CODE_RADIX = 8
LANES = 128
SUBLANES = 8
MOE_SLOTS = 4
MOE_SUPER = 2048


def _silu(v):
    return v * jax.nn.sigmoid(v)


def _rms_mod(x, g, scale, shift):
    ms = jnp.mean(x * x, axis=-1, keepdims=True)
    y = (x * lax.rsqrt(ms + EPS)) * g
    return y * (1.0 + scale) + shift


def _split(v):
    hi = v.astype(BF16)
    return hi, (v - hi.astype(F32)).astype(BF16)


def _dot(a, w_hi, w_lo=None):
    a_hi, a_lo = _split(a)
    out = jnp.dot(a_hi, w_hi, preferred_element_type=F32)
    if w_lo is not None:
        out = out + jnp.dot(a_lo, w_hi, preferred_element_type=F32)
        out = out + jnp.dot(a_hi, w_lo, preferred_element_type=F32)
    return out


def _dot_ref(a, w_ref):
    a_hi, a_lo = _split(a)
    if w_ref.shape[0] == 3 * a.shape[1]:
        a_hi = jnp.concatenate([a_hi, a_lo, a_hi], axis=1)
    return jnp.dot(a_hi, w_ref[...], preferred_element_type=F32)


def _mod_parts(mod):
    return [mod[..., i * D_MODEL:(i + 1) * D_MODEL] for i in range(N_MOD)]


def _ada_kernel(c_ref, w_ref, b_ref, o_ref):
    w_hi, w_lo = _split(w_ref[0])
    o_ref[0] = _dot(_silu(c_ref[...]), w_hi, w_lo) + b_ref[0]


def _ada(c_all, w_ada, b_ada):
    depth, _, n_out = w_ada.shape
    rows = c_all.shape[0]
    tn = 1536
    return pl.pallas_call(
        _ada_kernel,
        grid=(depth, n_out // tn),
        in_specs=[
            pl.BlockSpec((rows, D_MODEL), lambda l, n: (0, 0)),
            pl.BlockSpec((1, D_MODEL, tn), lambda l, n: (l, 0, n)),
            pl.BlockSpec((1, 1, tn), lambda l, n: (l, 0, n)),
        ],
        out_specs=pl.BlockSpec((1, rows, tn), lambda l, n: (l, 0, n)),
        out_shape=jax.ShapeDtypeStruct((depth, rows, n_out), F32),
        compiler_params=pltpu.CompilerParams(
            dimension_semantics=("arbitrary", "arbitrary"), vmem_limit_bytes=VMEM_LIMIT_BYTES),
        name="ada",
    )(c_all, w_ada, b_ada.reshape(depth, 1, n_out))


def _layer_norm(v, g, b):
    mu = jnp.mean(v, axis=-1, keepdims=True)
    d = v - mu
    var = jnp.mean(d * d, axis=-1, keepdims=True)
    return d * lax.rsqrt(var + EPS) * g + b


def _pool_select(sums, counts, p):
    lane = lax.broadcasted_iota(jnp.int32, p.shape, p.ndim - 1)
    pooled = sums[-1] / counts[-1]
    for gi in range(len(POOL_WINDOWS) - 2, -1, -1):
        pooled = jnp.where(lane < (gi + 1) * C_GROUP, sums[gi] / counts[gi], pooled)
    return pooled - p


def _router_logits_t(h2, h2_hi, wr_hi, wr_lo):
    h2_lo = (h2 - h2_hi.astype(F32)).astype(BF16)
    nt = (((1,), (1,)), ((), ()))
    both = lax.dot_general(jnp.concatenate([wr_hi, wr_lo], axis=0), h2_hi, nt,
                           preferred_element_type=F32)
    lt = both[:N_EXPERTS] + lax.dot_general(wr_hi, h2_lo, nt, preferred_element_type=F32)
    return lt + both[N_EXPERTS:]


def _mixer_prompt_kernel(has_prev, tile, *refs):
    if has_prev:
        x_ref, f_ref, modp_ref, *refs = refs
    else:
        x_ref, *refs = refs
    (mod_ref, ng_ref, win_ref, caw_ref, cbw_ref, cbb_ref, lng_ref, lnb_ref, pw_ref, ps_ref,
     wout_ref, nfg_ref, wrh_ref, wrl_ref,
     xmid_ref, h2_ref, lt_ref, na_ref, nb_ref, nc_ref,
     va_ref, ub_ref, pc_ref, ubs_ref, pcs_ref) = refs

    s = pl.program_id(1)

    @pl.when(s == 0)
    def _():
        va_ref[0:HALO_A, :] = jnp.zeros((HALO_A, D_A), F32)
        ub_ref[0:HALO_B, :] = jnp.zeros((HALO_B, D_B), F32)
        pc_ref[0:HALO_C, :] = jnp.zeros((HALO_C, D_C), F32)

    x = x_ref[0]
    if has_prev:
        x = x + _mod_parts(modp_ref[0])[5] * f_ref[0]
    sh_m, sc_m, g_m, sh_f, sc_f, g_f = _mod_parts(mod_ref[0])

    proj = _dot_ref(_rms_mod(x, ng_ref[...], sc_m, sh_m), win_ref)
    xa = proj[:, 0:D_A]
    ca = proj[:, D_A:2 * D_A]
    ba = proj[:, 2 * D_A:3 * D_A]
    a_b = proj[:, 3 * D_A:3 * D_A + D_B]
    g_b = proj[:, 3 * D_A + D_B:3 * D_A + 2 * D_B]
    p_c = proj[:, 3 * D_A + 2 * D_B:]

    va_ref[HALO_A:HALO_A + tile, :] = ca * xa
    acc = caw_ref[0:1, :] * va_ref[HALO_A - 2:HALO_A - 2 + tile, :]
    for k in range(1, CONV_A):
        acc = acc + caw_ref[k:k + 1, :] * va_ref[HALO_A - 2 + k:HALO_A - 2 + k + tile, :]
    ya = ba * acc
    na_ref[0] = va_ref[HALO_A + tile - (CONV_A - 1):HALO_A + tile, :]

    def window_reader(src_ref, shifted_ref):
        span = shifted_ref.shape[1]
        for r in range(1, SUBLANES):
            shifted_ref[r - 1] = src_ref[r:r + span, :]

        def window(row):
            r = row % SUBLANES
            if r == 0:
                return src_ref[row:row + tile, :]
            return shifted_ref[r - 1, row - r:row - r + tile, :]
        return window

    ub_ref[HALO_B:HALO_B + tile, :] = a_b * jax.nn.sigmoid(g_b)
    window = window_reader(ub_ref, ubs_ref)
    off = HALO_B - (CONV_B - 1)
    acc = cbw_ref[0:1, :] * window(off)
    for k in range(1, CONV_B):
        acc = acc + cbw_ref[k:k + 1, :] * window(off + k)
    yb = _silu(_layer_norm(acc + cbb_ref[...], lng_ref[...], lnb_ref[...]))
    nb_ref[0] = ub_ref[HALO_B + tile - (CONV_B - 1):HALO_B + tile, :]

    pc_ref[HALO_C:HALO_C + tile, :] = p_c
    window = window_reader(pc_ref, pcs_ref)
    pos = s * tile + lax.broadcasted_iota(jnp.int32, (tile, D_C), 0)
    run = p_c
    sums, counts = [], []
    for j in range(1, POOL_WINDOWS[-1]):
        run = run + window(HALO_C - j)
        if j + 1 in POOL_WINDOWS:
            sums.append(run)
            counts.append(jnp.minimum(pos + 1, j + 1).astype(F32))
    d = _pool_select(sums, counts, p_c)
    yc = _dot_ref(d, pw_ref) * ps_ref[...]
    nc_ref[0] = pc_ref[HALO_C + tile - POOL_BUF:HALO_C + tile, :]

    va_ref[0:HALO_A, :] = va_ref[tile:tile + HALO_A, :]
    ub_ref[0:HALO_B, :] = ub_ref[tile:tile + HALO_B, :]
    pc_ref[0:HALO_C, :] = pc_ref[tile:tile + HALO_C, :]

    mix = jnp.concatenate([ya, yb, yc], axis=-1)
    xmid = x + g_m * _dot_ref(mix, wout_ref)
    xmid_ref[0] = xmid

    h2 = _rms_mod(xmid, nfg_ref[...], sc_f, sh_f)
    h2_hi = h2.astype(BF16)
    h2_ref[0] = h2_hi
    lt_ref[...] = _router_logits_t(h2, h2_hi, wrh_ref[...], wrl_ref[...])


def _full(shape):
    return pl.BlockSpec(shape, lambda *_: (0,) * len(shape))


def _resident(shape):
    return pl.BlockSpec(shape, lambda *_: (0,) * len(shape), pipeline_mode=pl.Buffered(1))


def _mixer_prompt(x, prev, mod, lw):
    nb, seq, _ = x.shape
    tile = min(MIX_TILE, seq)
    nt = seq // tile
    tok = pl.BlockSpec((1, tile, D_MODEL), lambda b, s: (b, s, 0))
    modspec = pl.BlockSpec((1, 1, N_MOD * D_MODEL), lambda b, s: (b, 0, 0))
    args, specs = [x], [tok]
    if prev is not None:
        args += [prev[0], prev[1]]
        specs += [tok, modspec]
    args += [mod, lw["norm_mix_g"], lw["w_in"], lw["conv_a_w"], lw["conv_b_w"], lw["conv_b_bias"],
             lw["ln_b_g"], lw["ln_b_b"], lw["pool_w"], lw["pool_scale"], lw["w_out"],
             lw["norm_ffn_g"], lw["wr_hi"], lw["wr_lo"]]
    specs += [modspec] + [_resident(a.shape) for a in args[len(specs) + 1:]]
    out_shape = (
        jax.ShapeDtypeStruct((nb, seq, D_MODEL), F32),
        jax.ShapeDtypeStruct((nb, seq, D_MODEL), BF16),
        jax.ShapeDtypeStruct((N_EXPERTS, nb * seq), F32),
        jax.ShapeDtypeStruct((nb, CONV_A - 1, D_A), F32),
        jax.ShapeDtypeStruct((nb, CONV_B - 1, D_B), F32),
        jax.ShapeDtypeStruct((nb, POOL_BUF, D_C), F32),
    )
    out_specs = (
        tok, tok,
        pl.BlockSpec((N_EXPERTS, tile), lambda b, s: (0, b * nt + s)),
        pl.BlockSpec((1, CONV_A - 1, D_A), lambda b, s: (b, 0, 0)),
        pl.BlockSpec((1, CONV_B - 1, D_B), lambda b, s: (b, 0, 0)),
        pl.BlockSpec((1, POOL_BUF, D_C), lambda b, s: (b, 0, 0)),
    )
    return pl.pallas_call(
        functools.partial(_mixer_prompt_kernel, prev is not None, tile),
        grid=(nb, nt),
        in_specs=specs,
        out_specs=out_specs,
        out_shape=out_shape,
        scratch_shapes=[
            pltpu.VMEM((HALO_A + tile, D_A), F32),
            pltpu.VMEM((HALO_B + tile, D_B), F32),
            pltpu.VMEM((HALO_C + tile, D_C), F32),
            pltpu.VMEM((SUBLANES - 1, HALO_B - SUBLANES + tile, D_B), F32),
            pltpu.VMEM((SUBLANES - 1, HALO_C - SUBLANES + tile, D_C), F32),
        ],
        compiler_params=pltpu.CompilerParams(
            dimension_semantics=("arbitrary", "arbitrary"), vmem_limit_bytes=VMEM_LIMIT_BYTES),
        name="mixer_prompt",
    )(*args)


def _mixer_sample_kernel(has_prev, *refs):
    if has_prev:
        x_ref, f_ref, modp_ref, *refs = refs
    else:
        x_ref, *refs = refs
    (mod_ref, sa_ref, sb_ref, sc_ref, ng_ref, win_ref, caw_ref, cbw_ref, cbb_ref, lng_ref, lnb_ref,
     pw_ref, ps_ref, wout_ref, nfg_ref, wrh_ref, wrl_ref,
     xmid_ref, h2_ref, lt_ref, na_ref, nb_ref, nc_ref) = refs

    t_new, n_seq, _ = x_ref.shape
    rows = t_new * n_seq
    x = x_ref[...]
    if has_prev:
        x = x + _mod_parts(modp_ref[...])[5][None] * f_ref[...]
    sh_m, sc_m, g_m, sh_f, sc_f, g_f = [m[None] for m in _mod_parts(mod_ref[...])]

    h = _rms_mod(x, ng_ref[...], sc_m, sh_m)
    proj = _dot_ref(h.reshape(rows, D_MODEL), win_ref).reshape(t_new, n_seq, D_IN)
    xa = proj[:, :, 0:D_A]
    ca = proj[:, :, D_A:2 * D_A]
    ba = proj[:, :, 2 * D_A:3 * D_A]
    a_b = proj[:, :, 3 * D_A:3 * D_A + D_B]
    g_b = proj[:, :, 3 * D_A + D_B:3 * D_A + 2 * D_B]
    p_c = proj[:, :, 3 * D_A + 2 * D_B:]

    def history(state_ref, new):
        return [state_ref[j] for j in range(state_ref.shape[0])] + [new[t] for t in range(t_new)]

    def conv(seq, w_ref, taps):
        outs = []
        for t in range(t_new):
            acc = w_ref[0:1, :] * seq[t]
            for k in range(1, taps):
                acc = acc + w_ref[k:k + 1, :] * seq[t + k]
            outs.append(acc)
        return jnp.stack(outs)

    def emit_state(out_ref, seq):
        keep = out_ref.shape[0]
        for j in range(keep):
            out_ref[j] = seq[len(seq) - keep + j]

    seq_a = history(sa_ref, ca * xa)
    ya = ba * conv(seq_a, caw_ref, CONV_A)
    emit_state(na_ref, seq_a)

    seq_b = history(sb_ref, a_b * jax.nn.sigmoid(g_b))
    vb = conv(seq_b, cbw_ref, CONV_B) + cbb_ref[...]
    yb = _silu(_layer_norm(vb, lng_ref[...], lnb_ref[...]))
    emit_state(nb_ref, seq_b)

    seq_c = history(sc_ref, p_c)
    pooled = []
    for t in range(t_new):
        cur = POOL_BUF + t
        run = seq_c[cur]
        sums, counts = [], []
        for j in range(1, POOL_WINDOWS[-1]):
            run = run + seq_c[cur - j]
            if j + 1 in POOL_WINDOWS:
                sums.append(run)
                counts.append(float(min(PAST_LEN + t + 1, j + 1)))
        pooled.append(_pool_select(sums, counts, seq_c[cur]))
    d = jnp.stack(pooled)
    yc = _dot_ref(d.reshape(rows, D_C), pw_ref).reshape(t_new, n_seq, D_C) * ps_ref[...]
    emit_state(nc_ref, seq_c)

    mix = jnp.concatenate([ya, yb, yc], axis=-1).reshape(rows, D_MODEL)
    xmid = x + g_m * _dot_ref(mix, wout_ref).reshape(t_new, n_seq, D_MODEL)
    xmid_ref[...] = xmid

    h2 = _rms_mod(xmid, nfg_ref[...], sc_f, sh_f).reshape(rows, D_MODEL)
    h2_hi = h2.astype(BF16)
    h2_ref[...] = h2_hi
    lt_ref[...] = _router_logits_t(h2, h2_hi, wrh_ref[...], wrl_ref[...])


def _mixer_sample(x, prev, mod, states, lw):
    t_new, n_seq, _ = x.shape
    rows = t_new * n_seq
    args = [x]
    if prev is not None:
        args += [prev[0], prev[1]]
    args += [mod, *states, lw["norm_mix_g"], lw["w_in"], lw["conv_a_w"], lw["conv_b_w"],
             lw["conv_b_bias"], lw["ln_b_g"], lw["ln_b_b"], lw["pool_w"], lw["pool_scale"],
             lw["w_out"], lw["norm_ffn_g"], lw["wr_hi"], lw["wr_lo"]]
    out_shape = (
        jax.ShapeDtypeStruct((t_new, n_seq, D_MODEL), F32),
        jax.ShapeDtypeStruct((rows, D_MODEL), BF16),
        jax.ShapeDtypeStruct((N_EXPERTS, rows), F32),
        jax.ShapeDtypeStruct((CONV_A - 1, n_seq, D_A), F32),
        jax.ShapeDtypeStruct((CONV_B - 1, n_seq, D_B), F32),
        jax.ShapeDtypeStruct((POOL_BUF, n_seq, D_C), F32),
    )
    return pl.pallas_call(
        functools.partial(_mixer_sample_kernel, prev is not None),
        grid=(1,),
        in_specs=[_full(a.shape) for a in args],
        out_specs=tuple(_full(o.shape) for o in out_shape),
        out_shape=out_shape,
        compiler_params=pltpu.CompilerParams(
            dimension_semantics=("arbitrary",), vmem_limit_bytes=VMEM_LIMIT_BYTES),
        name="mixer_sample",
    )(*args)


def _route(lt, bias):
    n_tok = lt.shape[1]
    scores = jax.nn.sigmoid(lt)
    biased = scores + bias
    sj = [scores[GROUP_SIZE * j:GROUP_SIZE * (j + 1)] for j in range(GROUP_SIZE)]
    bj = [biased[GROUP_SIZE * j:GROUP_SIZE * (j + 1)] for j in range(GROUP_SIZE)]

    m1 = bj[0]
    m2 = jnp.full_like(m1, -jnp.inf)
    for j in range(1, GROUP_SIZE):
        m2 = jnp.maximum(m2, jnp.minimum(m1, bj[j]))
        m1 = jnp.maximum(m1, bj[j])
    gscore = m1 + m2

    gidx = lax.broadcasted_iota(jnp.int32, (N_EXPERT_GROUPS, n_tok), 0)
    lower = [None] + [jnp.where(gidx >= k, gidx - k, gidx - k + N_EXPERT_GROUPS) < gidx
                      for k in range(1, N_EXPERT_GROUPS)]

    def beats(other, mine, tie_wins):
        return (other > mine) | ((other == mine) & tie_wins)

    grank = jnp.zeros((N_EXPERT_GROUPS, n_tok), jnp.int32)
    for k in range(1, N_EXPERT_GROUPS):
        grank = grank + beats(pltpu.roll(gscore, k, 0), gscore, lower[k]).astype(jnp.int32)
    gsel = grank < TOPK_GROUPS

    def over_groups(op, v):
        for k in (4, 2, 1):
            v = op(v, pltpu.roll(v, k, 0))
        return v

    mj = [jnp.where(gsel, b, -jnp.inf) for b in bj]
    eid = [(gidx * GROUP_SIZE + j).astype(F32) for j in range(GROUP_SIZE)]
    sel = [jnp.zeros((N_EXPERT_GROUPS, n_tok), F32) for _ in range(GROUP_SIZE)]
    for _ in range(TOP_K):
        top = over_groups(jnp.maximum, functools.reduce(jnp.maximum, mj))
        first = over_groups(jnp.minimum, functools.reduce(
            jnp.minimum,
            [jnp.where(mj[j] == top, eid[j], float(N_EXPERTS)) for j in range(GROUP_SIZE)]))
        for j in range(GROUP_SIZE):
            hit = eid[j] == first
            sel[j] = jnp.where(hit, 1.0, sel[j])
            mj[j] = jnp.where(hit, -jnp.inf, mj[j])

    picked = [sel[j] * sj[j] for j in range(GROUP_SIZE)]
    tot = picked[0]
    for j in range(1, GROUP_SIZE):
        tot = tot + picked[j]
    denom = jnp.sum(tot, axis=0, keepdims=True)
    gates = [picked[j] / denom * ROUTED_SCALE for j in range(GROUP_SIZE)]
    return jnp.concatenate(sel, axis=0), jnp.concatenate(gates, axis=0)


def _moe_kernel(n_blocks, h2_ref, lt_ref, rb_ref, wg_ref, wu_ref, wd_ref, wsg_ref, wsu_ref,
                wsd_ref, o_ref, rank_ref, gate_ref, p_ref, xs_ref, ys_ref, gs_ref, codes_ref):
    q = pl.program_id(1)
    seg = MOE_CAP
    blk = MOE_BLOCK

    @pl.when(q == 0)
    def _():
        sel, gates = _route(lt_ref[...], rb_ref[...])
        gate_ref[...] = gates
        before = jnp.where(lax.broadcasted_iota(jnp.int32, (blk, blk), 0)
                           < lax.broadcasted_iota(jnp.int32, (blk, blk), 1), 1.0, 0.0).astype(BF16)
        passes = []
        for b in range(n_blocks):
            sb = sel[:, b * blk:(b + 1) * blk]
            cnt = jnp.dot(sb.astype(BF16), before, preferred_element_type=F32)
            rank_ref[:, b * blk:(b + 1) * blk] = jnp.where(sb > 0.5, cnt, -1.0)
            total = jnp.broadcast_to(jnp.max((cnt + 1.0) * sb, axis=1, keepdims=True),
                                     (N_EXPERTS, LANES))
            passes.append(((total + (seg - 1)) * (1.0 / seg)).astype(jnp.int32))
        row = lax.broadcasted_iota(jnp.int32, (N_EXPERT_GROUPS, LANES), 0) % MOE_SLOTS
        field = functools.reduce(lambda a, s: jnp.where(row == s, CODE_RADIX ** s, a),
                                 range(1, MOE_SLOTS), jnp.ones_like(row))

        def over_slots(op, v):
            v = op(v, pltpu.roll(v, 1, 0))
            return op(v, pltpu.roll(v, 2, 0))

        for j in range(GROUP_SIZE):
            slab = [p[N_EXPERT_GROUPS * j:N_EXPERT_GROUPS * (j + 1)] for p in passes]
            slot_code = over_slots(jnp.add, functools.reduce(jnp.maximum, slab) * field)
            blk_code = functools.reduce(
                jnp.add, [over_slots(jnp.maximum, slab[b]) * CODE_RADIX ** b for b in range(n_blocks)])
            for i in range(N_EXPERT_GROUPS // MOE_SLOTS):
                last = MOE_SLOTS * (i + 1) - 1
                step = (N_EXPERT_GROUPS // MOE_SLOTS) * j + i
                codes_ref[step] = jnp.max(slot_code[last:last + 1])
                codes_ref[N_EXPERTS // MOE_SLOTS + step] = jnp.max(blk_code[last:last + 1])
        x = h2_ref[...]
        hid = _silu(jnp.dot(x, wsg_ref[...], preferred_element_type=F32)) * jnp.dot(
            x, wsu_ref[...], preferred_element_type=F32)
        o_ref[...] = jnp.dot(hid.astype(BF16), wsd_ref[...], preferred_element_type=F32)

    r0 = q * MOE_SLOTS
    pos = lax.broadcasted_iota(jnp.int32, (seg, blk), 0).astype(F32)

    def passes_of(code, i):
        return lax.shift_right_logical(code, 3 * i) & (CODE_RADIX - 1)

    slot_passes = [passes_of(codes_ref[q], s) for s in range(MOE_SLOTS)]
    blk_passes = [passes_of(codes_ref[N_EXPERTS // MOE_SLOTS + q], b) for b in range(n_blocks)]

    def gather(b, p):
        base = (p * seg).astype(F32) if not isinstance(p, int) else float(p * seg)
        rows = []
        for s in range(MOE_SLOTS):
            rrow = rank_ref[pl.ds(r0 + s, 1), b * blk:(b + 1) * blk]
            grow = gate_ref[pl.ds(r0 + s, 1), b * blk:(b + 1) * blk]
            match = (rrow - base) == pos
            rows.append(jnp.where(match, 1.0, 0.0).astype(BF16))
            gs_ref[s, b] = jnp.sum(jnp.where(match, grow, 0.0), axis=1, keepdims=True)
        onehot = jnp.concatenate(rows, axis=0)
        p_ref[b] = onehot
        xs = jnp.dot(onehot, h2_ref[b * blk:(b + 1) * blk, :],
                     preferred_element_type=F32).astype(BF16)
        for s in range(MOE_SLOTS):
            xs_ref[s, b] = xs[s * seg:(s + 1) * seg]

    def expert_mlp(s):
        lhs = xs_ref[s].reshape(n_blocks * seg, D_MODEL)
        hid = _silu(jnp.dot(lhs, wg_ref[s, 0].astype(BF16), preferred_element_type=F32)) * jnp.dot(
            lhs, wu_ref[s, 0].astype(BF16), preferred_element_type=F32)
        y = jnp.dot(hid.astype(BF16), wd_ref[s, 0].astype(BF16), preferred_element_type=F32)
        y = gs_ref[s].reshape(n_blocks * seg, 1) * y
        ys_ref[s] = y.astype(BF16).reshape(n_blocks, seg, D_MODEL)

    def combine(b):
        yb = jnp.concatenate([ys_ref[s, b] for s in range(MOE_SLOTS)], axis=0)
        o_ref[b * blk:(b + 1) * blk, :] += lax.dot_general(
            p_ref[b], yb, (((0,), (0,)), ((), ())), preferred_element_type=F32)

    for b in range(n_blocks):
        gather(b, 0)
    for s in range(MOE_SLOTS):
        expert_mlp(s)
    for b in range(n_blocks):
        combine(b)

    def overflow_pass(p, carry):
        for b in range(n_blocks):
            pl.when(blk_passes[b] > p)(functools.partial(gather, b, p))
        for s in range(MOE_SLOTS):
            pl.when(slot_passes[s] > p)(functools.partial(expert_mlp, s))
        for b in range(n_blocks):
            pl.when(blk_passes[b] > p)(functools.partial(combine, b))
        return carry

    lax.fori_loop(1, functools.reduce(jnp.maximum, slot_passes), overflow_pass, 0)


def _moe(h2, lt, lw):
    n_tok = h2.shape[0]
    sup = min(MOE_SUPER, n_tok)
    n_blocks = sup // MOE_BLOCK
    n_steps = N_EXPERTS // MOE_SLOTS
    half = N_EXPERT_GROUPS // MOE_SLOTS
    layer = lw["layer"]
    assert n_blocks * 3 <= 30, "block pass counts are packed as 3-bit fields of one int32"
    wspec = lambda a, b: pl.BlockSpec(
        (None, MOE_SLOTS, 1, a, b), lambda t, q: (layer, q % half, q // half, 0, 0))
    return pl.pallas_call(
        functools.partial(_moe_kernel, n_blocks),
        grid=(n_tok // sup, n_steps),
        in_specs=[
            pl.BlockSpec((sup, D_MODEL), lambda t, q: (t, 0), pipeline_mode=pl.Buffered(1)),
            pl.BlockSpec((N_EXPERTS, sup), lambda t, q: (0, t), pipeline_mode=pl.Buffered(1)),
            _resident((N_EXPERTS, 1)),
            wspec(D_MODEL, D_EXPERT), wspec(D_MODEL, D_EXPERT), wspec(D_EXPERT, D_MODEL),
            _resident(lw["ws_gate"].shape), _resident(lw["ws_up"].shape),
            _resident(lw["ws_down"].shape),
        ],
        out_specs=pl.BlockSpec((sup, D_MODEL), lambda t, q: (t, 0), pipeline_mode=pl.Buffered(1)),
        out_shape=jax.ShapeDtypeStruct((n_tok, D_MODEL), F32),
        scratch_shapes=[
            pltpu.VMEM((N_EXPERTS, sup), F32),
            pltpu.VMEM((N_EXPERTS, sup), F32),
            pltpu.VMEM((n_blocks, MOE_SLOTS * MOE_CAP, MOE_BLOCK), BF16),
            pltpu.VMEM((MOE_SLOTS, n_blocks, MOE_CAP, D_MODEL), BF16),
            pltpu.VMEM((MOE_SLOTS, n_blocks, MOE_CAP, D_MODEL), BF16),
            pltpu.VMEM((MOE_SLOTS, n_blocks, MOE_CAP, 1), F32),
            pltpu.SMEM((2 * n_steps,), jnp.int32),
        ],
        compiler_params=pltpu.CompilerParams(
            dimension_semantics=("arbitrary", "arbitrary"), vmem_limit_bytes=VMEM_LIMIT_BYTES),
        name="moe",
    )(h2, lt, lw["router_bias"], lw["w_gate"], lw["w_up"], lw["w_down"],
      lw["ws_gate"], lw["ws_up"], lw["ws_down"])


def _final_kernel(x_ref, f_ref, mod_ref, g_ref, o_ref):
    gate = _mod_parts(mod_ref[...])[5]
    if x_ref.ndim == 3 and mod_ref.ndim == 2:
        gate = gate[None]
    x = x_ref[...] + gate * f_ref[...]
    ms = jnp.mean(x * x, axis=-1, keepdims=True)
    o_ref[...] = (x * lax.rsqrt(ms + EPS)) * g_ref[...]


def _final_prompt(xmid, ffn, mod, g):
    nb, seq, _ = xmid.shape
    tile = min(MIX_TILE, seq)
    tok = pl.BlockSpec((1, tile, D_MODEL), lambda b, s: (b, s, 0))
    return pl.pallas_call(
        _final_kernel,
        grid=(nb, seq // tile),
        in_specs=[tok, tok, pl.BlockSpec((1, 1, N_MOD * D_MODEL), lambda b, s: (b, 0, 0)),
                  _full(g.shape)],
        out_specs=tok,
        out_shape=jax.ShapeDtypeStruct(xmid.shape, F32),
        compiler_params=pltpu.CompilerParams(
            dimension_semantics=("arbitrary", "arbitrary"), vmem_limit_bytes=VMEM_LIMIT_BYTES),
        name="final_prompt",
    )(xmid, ffn, mod, g)


def _final_sample(xmid, ffn, mod, g):
    return pl.pallas_call(
        _final_kernel,
        grid=(1,),
        in_specs=[_full(xmid.shape), _full(ffn.shape), _full(mod.shape), _full(g.shape)],
        out_specs=_full(xmid.shape),
        out_shape=jax.ShapeDtypeStruct(xmid.shape, F32),
        compiler_params=pltpu.CompilerParams(
            dimension_semantics=("arbitrary",), vmem_limit_bytes=VMEM_LIMIT_BYTES),
        name="final_sample",
    )(xmid, ffn, mod, g)


def _slot_order(a, axis):
    shape = a.shape
    a = a.reshape(shape[:axis] + (N_EXPERT_GROUPS, GROUP_SIZE) + shape[axis + 1:])
    return jnp.swapaxes(a, axis, axis + 1).reshape(shape)


def _layer_weights(l, w_in, norm_mix_g, conv_a_w, conv_b_w, conv_b_bias, ln_b_g, ln_b_b, pool_w,
                   pool_scale, w_out, norm_ffn_g, w_router, router_bias, w_gate, w_up, w_down,
                   ws_gate, ws_up, ws_down):
    row = lambda v: v[l].reshape(1, -1)
    wr = _slot_order(w_router[l], 1).T
    wr_hi = wr.astype(BF16)
    blockdiag = jax.scipy.linalg.block_diag(*[pool_w[l, g] for g in range(pool_w.shape[1])])
    precise = l + 1 < w_in.shape[0]
    def stacked(w):
        hi, lo = _split(w)
        return jnp.concatenate([hi, hi, lo], axis=0) if precise else hi
    return dict(
        norm_mix_g=row(norm_mix_g), w_in=stacked(w_in[l]), conv_a_w=conv_a_w[l],
        conv_b_w=conv_b_w[l], conv_b_bias=row(conv_b_bias), ln_b_g=row(ln_b_g), ln_b_b=row(ln_b_b),
        pool_w=stacked(blockdiag), pool_scale=row(pool_scale), w_out=stacked(w_out[l]),
        norm_ffn_g=row(norm_ffn_g), wr_hi=wr_hi, wr_lo=(wr - wr_hi.astype(F32)).astype(BF16),
        router_bias=_slot_order(router_bias[l], 0).reshape(N_EXPERTS, 1),
        layer=l, w_gate=w_gate, w_up=w_up, w_down=w_down,
        ws_gate=ws_gate[l].astype(BF16), ws_up=ws_up[l].astype(BF16), ws_down=ws_down[l].astype(BF16),
    )


def kernel(x_prompt, x_sample, c_prompt, c_sample, state_conv_a, state_conv_b, state_pool, w_ada, b_ada, norm_mix_g, w_in, conv_a_w, conv_b_w, conv_b_bias, ln_b_g, ln_b_b, pool_w, pool_scale, w_out, norm_ffn_g, w_router, router_bias, w_gate, w_up, w_down, ws_gate, ws_up, ws_down, final_norm_g):
    depth = w_ada.shape[0]
    n_p, seq, _ = x_prompt.shape
    n_s, t_new, _ = x_sample.shape

    mod = _ada(jnp.concatenate([c_prompt, c_sample], axis=0), w_ada, b_ada)
    mod_p = mod[:, :n_p].reshape(depth, n_p, 1, N_MOD * D_MODEL)
    mod_s = mod[:, n_p:]

    xp = x_prompt
    xs = jnp.swapaxes(x_sample, 0, 1)
    time_major = lambda st: jnp.swapaxes(st, 1, 2)
    st_a, st_b, st_c = time_major(state_conv_a), time_major(state_conv_b), time_major(state_pool)

    experts = lambda w: w.reshape((depth, N_EXPERT_GROUPS, GROUP_SIZE) + w.shape[2:])
    w_gate, w_up, w_down = experts(w_gate), experts(w_up), experts(w_down)

    prev_p = prev_s = None
    new_p, new_s = [], []
    for l in range(depth):
        lw = _layer_weights(l, w_in, norm_mix_g, conv_a_w, conv_b_w, conv_b_bias, ln_b_g, ln_b_b,
                            pool_w, pool_scale, w_out, norm_ffn_g, w_router, router_bias,
                            w_gate, w_up, w_down, ws_gate, ws_up, ws_down)
        xp, h2p, ltp, na, nb, nc = _mixer_prompt(xp, prev_p, mod_p[l], lw)
        new_p.append((na, nb, nc))
        ffn_p = _moe(h2p.reshape(n_p * seq, D_MODEL), ltp, lw).reshape(n_p, seq, D_MODEL)
        prev_p = (ffn_p, mod_p[l])

        xs, h2s, lts, na, nb, nc = _mixer_sample(xs, prev_s, mod_s[l], (st_a[l], st_b[l], st_c[l]), lw)
        new_s.append((na, nb, nc))
        ffn_s = _moe(h2s, lts, lw).reshape(t_new, n_s, D_MODEL)
        prev_s = (ffn_s, mod_s[l])

    g = final_norm_g.reshape(1, D_MODEL)
    y_prompt = _final_prompt(xp, prev_p[0], prev_p[1], g)
    y_sample = jnp.swapaxes(_final_sample(xs, prev_s[0], prev_s[1], g), 0, 1)

    stack = lambda items, i: jnp.stack([it[i] for it in items])
    batch_major = lambda a: jnp.swapaxes(a, 1, 2)
    return (y_prompt, y_sample,
            stack(new_p, 0), stack(new_p, 1), stack(new_p, 2),
            batch_major(stack(new_s, 0)), batch_major(stack(new_s, 1)), batch_major(stack(new_s, 2)))
```

```python
import functools

import jax
import jax.numpy as jnp
from jax import lax
from jax.experimental import pallas as pl
from jax.experimental.pallas import tpu as pltpu

D_MODEL = 1024
HEAD_DIM = 64
D_A = 6 * HEAD_DIM
D_B = 6 * HEAD_DIM
D_C = D_MODEL - D_A - D_B
C_GROUP = 64
POOL_WINDOWS = (2, 4, 8, 16)
POOL_BUF = max(POOL_WINDOWS) - 1
CONV_A = 3
CONV_B = 31
D_IN = 3 * D_A + 2 * D_B + D_C
N_EXPERTS = 64
N_EXPERT_GROUPS = 8
GROUP_SIZE = N_EXPERTS // N_EXPERT_GROUPS
TOPK_GROUPS = 4
TOP_K = 8
D_EXPERT = 256
ROUTED_SCALE = 2.5
N_MOD = 6
EPS = 1e-6
PAST_LEN = 16384

F32 = jnp.float32
BF16 = jnp.bfloat16

VMEM_LIMIT_BYTES = 56 * 1024 * 1024
MOE_VMEM_LIMIT_BYTES = 60 * 1024 * 1024

MIX_TILE = 512
HALO_A = 8
HALO_B = 32
HALO_C = 16

MOE_BLOCK = 256
MOE_CAP = 64
CODE_RADIX = 8
LANES = 128
SUBLANES = 8
MOE_SLOTS = 4
MOE_SUPER = 2048


def _silu(v):
    return v * jax.nn.sigmoid(v)


def _rms_mod(x, g, scale, shift):
    ms = jnp.mean(x * x, axis=-1, keepdims=True)
    y = (x * lax.rsqrt(ms + EPS)) * g
    return y * (1.0 + scale) + shift


def _split(v):
    hi = v.astype(BF16)
    return hi, (v - hi.astype(F32)).astype(BF16)


def _dot(a, w_hi, w_lo=None):
    a_hi, a_lo = _split(a)
    out = jnp.dot(a_hi, w_hi, preferred_element_type=F32)
    if w_lo is not None:
        out = out + jnp.dot(a_lo, w_hi, preferred_element_type=F32)
        out = out + jnp.dot(a_hi, w_lo, preferred_element_type=F32)
    return out


def _dot_ref(a, w_ref):
    a_hi, a_lo = _split(a)
    if w_ref.shape[0] == 3 * a.shape[1]:
        a_hi = jnp.concatenate([a_hi, a_lo, a_hi], axis=1)
    return jnp.dot(a_hi, w_ref[...], preferred_element_type=F32)


def _mod_parts(mod):
    return [mod[..., i * D_MODEL:(i + 1) * D_MODEL] for i in range(N_MOD)]


def _ada_kernel(c_ref, w_ref, b_ref, o_ref):
    w_hi, w_lo = _split(w_ref[0])
    o_ref[0] = _dot(_silu(c_ref[...]), w_hi, w_lo) + b_ref[0]


def _ada(c_all, w_ada, b_ada):
    depth, _, n_out = w_ada.shape
    rows = c_all.shape[0]
    tn = 1536
    return pl.pallas_call(
        _ada_kernel,
        grid=(depth, n_out // tn),
        in_specs=[
            pl.BlockSpec((rows, D_MODEL), lambda l, n: (0, 0)),
            pl.BlockSpec((1, D_MODEL, tn), lambda l, n: (l, 0, n)),
            pl.BlockSpec((1, 1, tn), lambda l, n: (l, 0, n)),
        ],
        out_specs=pl.BlockSpec((1, rows, tn), lambda l, n: (l, 0, n)),
        out_shape=jax.ShapeDtypeStruct((depth, rows, n_out), F32),
        compiler_params=pltpu.CompilerParams(
            dimension_semantics=("arbitrary", "arbitrary"), vmem_limit_bytes=VMEM_LIMIT_BYTES),
        name="ada",
    )(c_all, w_ada, b_ada.reshape(depth, 1, n_out))


def _layer_norm(v, g, b):
    mu = jnp.mean(v, axis=-1, keepdims=True)
    d = v - mu
    var = jnp.mean(d * d, axis=-1, keepdims=True)
    return d * lax.rsqrt(var + EPS) * g + b


def _pool_select(sums, counts, p):
    lane = lax.broadcasted_iota(jnp.int32, p.shape, p.ndim - 1)
    pooled = sums[-1] / counts[-1]
    for gi in range(len(POOL_WINDOWS) - 2, -1, -1):
        pooled = jnp.where(lane < (gi + 1) * C_GROUP, sums[gi] / counts[gi], pooled)
    return pooled - p


def _router_logits_t(h2, h2_hi, wr_hi, wr_lo):
    h2_lo = (h2 - h2_hi.astype(F32)).astype(BF16)
    nt = (((1,), (1,)), ((), ()))
    both = lax.dot_general(jnp.concatenate([wr_hi, wr_lo], axis=0), h2_hi, nt,
                           preferred_element_type=F32)
    lt = both[:N_EXPERTS] + lax.dot_general(wr_hi, h2_lo, nt, preferred_element_type=F32)
    return lt + both[N_EXPERTS:]


def _mixer_prompt_kernel(has_prev, tile, *refs):
    if has_prev:
        x_ref, f_ref, modp_ref, *refs = refs
    else:
        x_ref, *refs = refs
    (mod_ref, ng_ref, win_ref, caw_ref, cbw_ref, cbb_ref, lng_ref, lnb_ref, pw_ref, ps_ref,
     wout_ref, nfg_ref, wrh_ref, wrl_ref,
     xmid_ref, h2_ref, lt_ref, na_ref, nb_ref, nc_ref,
     va_ref, ub_ref, pc_ref, ubs_ref, pcs_ref) = refs

    s = pl.program_id(1)

    @pl.when(s == 0)
    def _():
        va_ref[0:HALO_A, :] = jnp.zeros((HALO_A, D_A), F32)
        ub_ref[0:HALO_B, :] = jnp.zeros((HALO_B, D_B), F32)
        pc_ref[0:HALO_C, :] = jnp.zeros((HALO_C, D_C), F32)

    x = x_ref[0]
    if has_prev:
        x = x + _mod_parts(modp_ref[0])[5] * f_ref[0]
    sh_m, sc_m, g_m, sh_f, sc_f, g_f = _mod_parts(mod_ref[0])

    proj = _dot_ref(_rms_mod(x, ng_ref[...], sc_m, sh_m), win_ref)
    xa = proj[:, 0:D_A]
    ca = proj[:, D_A:2 * D_A]
    ba = proj[:, 2 * D_A:3 * D_A]
    a_b = proj[:, 3 * D_A:3 * D_A + D_B]
    g_b = proj[:, 3 * D_A + D_B:3 * D_A + 2 * D_B]
    p_c = proj[:, 3 * D_A + 2 * D_B:]

    va_ref[HALO_A:HALO_A + tile, :] = ca * xa
    acc = caw_ref[0:1, :] * va_ref[HALO_A - 2:HALO_A - 2 + tile, :]
    for k in range(1, CONV_A):
        acc = acc + caw_ref[k:k + 1, :] * va_ref[HALO_A - 2 + k:HALO_A - 2 + k + tile, :]
    ya = ba * acc
    na_ref[0] = va_ref[HALO_A + tile - (CONV_A - 1):HALO_A + tile, :]

    def window_reader(src_ref, shifted_ref):
        span = shifted_ref.shape[1]
        for r in range(1, SUBLANES):
            shifted_ref[r - 1] = src_ref[r:r + span, :]

        def window(row):
            r = row % SUBLANES
            if r == 0:
                return src_ref[row:row + tile, :]
            return shifted_ref[r - 1, row - r:row - r + tile, :]
        return window

    ub_ref[HALO_B:HALO_B + tile, :] = a_b * jax.nn.sigmoid(g_b)
    window = window_reader(ub_ref, ubs_ref)
    off = HALO_B - (CONV_B - 1)
    acc = cbw_ref[0:1, :] * window(off)
    for k in range(1, CONV_B):
        acc = acc + cbw_ref[k:k + 1, :] * window(off + k)
    yb = _silu(_layer_norm(acc + cbb_ref[...], lng_ref[...], lnb_ref[...]))
    nb_ref[0] = ub_ref[HALO_B + tile - (CONV_B - 1):HALO_B + tile, :]

    pc_ref[HALO_C:HALO_C + tile, :] = p_c
    window = window_reader(pc_ref, pcs_ref)
    pos = s * tile + lax.broadcasted_iota(jnp.int32, (tile, D_C), 0)
    run = p_c
    sums, counts = [], []
    for j in range(1, POOL_WINDOWS[-1]):
        run = run + window(HALO_C - j)
        if j + 1 in POOL_WINDOWS:
            sums.append(run)
            counts.append(jnp.minimum(pos + 1, j + 1).astype(F32))
    d = _pool_select(sums, counts, p_c)
    yc = _dot_ref(d, pw_ref) * ps_ref[...]
    nc_ref[0] = pc_ref[HALO_C + tile - POOL_BUF:HALO_C + tile, :]

    va_ref[0:HALO_A, :] = va_ref[tile:tile + HALO_A, :]
    ub_ref[0:HALO_B, :] = ub_ref[tile:tile + HALO_B, :]
    pc_ref[0:HALO_C, :] = pc_ref[tile:tile + HALO_C, :]

    mix = jnp.concatenate([ya, yb, yc], axis=-1)
    xmid = x + g_m * _dot_ref(mix, wout_ref)
    xmid_ref[0] = xmid

    h2 = _rms_mod(xmid, nfg_ref[...], sc_f, sh_f)
    h2_hi = h2.astype(BF16)
    h2_ref[0] = h2_hi
    lt_ref[...] = _router_logits_t(h2, h2_hi, wrh_ref[...], wrl_ref[...])


def _full(shape):
    return pl.BlockSpec(shape, lambda *_: (0,) * len(shape))


def _resident(shape):
    return pl.BlockSpec(shape, lambda *_: (0,) * len(shape), pipeline_mode=pl.Buffered(1))


def _mixer_prompt(x, prev, mod, lw):
    nb, seq, _ = x.shape
    tile = min(MIX_TILE, seq)
    nt = seq // tile
    tok = pl.BlockSpec((1, tile, D_MODEL), lambda b, s: (b, s, 0))
    modspec = pl.BlockSpec((1, 1, N_MOD * D_MODEL), lambda b, s: (b, 0, 0))
    args, specs = [x], [tok]
    if prev is not None:
        args += [prev[0], prev[1]]
        specs += [tok, modspec]
    args += [mod, lw["norm_mix_g"], lw["w_in"], lw["conv_a_w"], lw["conv_b_w"], lw["conv_b_bias"],
             lw["ln_b_g"], lw["ln_b_b"], lw["pool_w"], lw["pool_scale"], lw["w_out"],
             lw["norm_ffn_g"], lw["wr_hi"], lw["wr_lo"]]
    specs += [modspec] + [_resident(a.shape) for a in args[len(specs) + 1:]]
    out_shape = (
        jax.ShapeDtypeStruct((nb, seq, D_MODEL), F32),
        jax.ShapeDtypeStruct((nb, seq, D_MODEL), BF16),
        jax.ShapeDtypeStruct((N_EXPERTS, nb * seq), F32),
        jax.ShapeDtypeStruct((nb, CONV_A - 1, D_A), F32),
        jax.ShapeDtypeStruct((nb, CONV_B - 1, D_B), F32),
        jax.ShapeDtypeStruct((nb, POOL_BUF, D_C), F32),
    )
    out_specs = (
        tok, tok,
        pl.BlockSpec((N_EXPERTS, tile), lambda b, s: (0, b * nt + s)),
        pl.BlockSpec((1, CONV_A - 1, D_A), lambda b, s: (b, 0, 0)),
        pl.BlockSpec((1, CONV_B - 1, D_B), lambda b, s: (b, 0, 0)),
        pl.BlockSpec((1, POOL_BUF, D_C), lambda b, s: (b, 0, 0)),
    )
    return pl.pallas_call(
        functools.partial(_mixer_prompt_kernel, prev is not None, tile),
        grid=(nb, nt),
        in_specs=specs,
        out_specs=out_specs,
        out_shape=out_shape,
        scratch_shapes=[
            pltpu.VMEM((HALO_A + tile, D_A), F32),
            pltpu.VMEM((HALO_B + tile, D_B), F32),
            pltpu.VMEM((HALO_C + tile, D_C), F32),
            pltpu.VMEM((SUBLANES - 1, HALO_B - SUBLANES + tile, D_B), F32),
            pltpu.VMEM((SUBLANES - 1, HALO_C - SUBLANES + tile, D_C), F32),
        ],
        compiler_params=pltpu.CompilerParams(
            dimension_semantics=("arbitrary", "arbitrary"), vmem_limit_bytes=VMEM_LIMIT_BYTES),
        name="mixer_prompt",
    )(*args)


def _mixer_sample_kernel(has_prev, *refs):
    if has_prev:
        x_ref, f_ref, modp_ref, *refs = refs
    else:
        x_ref, *refs = refs
    (mod_ref, sa_ref, sb_ref, sc_ref, ng_ref, win_ref, caw_ref, cbw_ref, cbb_ref, lng_ref, lnb_ref,
     pw_ref, ps_ref, wout_ref, nfg_ref, wrh_ref, wrl_ref,
     xmid_ref, h2_ref, lt_ref, na_ref, nb_ref, nc_ref) = refs

    t_new, n_seq, _ = x_ref.shape
    rows = t_new * n_seq
    x = x_ref[...]
    if has_prev:
        x = x + _mod_parts(modp_ref[...])[5][None] * f_ref[...]
    sh_m, sc_m, g_m, sh_f, sc_f, g_f = [m[None] for m in _mod_parts(mod_ref[...])]

    h = _rms_mod(x, ng_ref[...], sc_m, sh_m)
    proj = _dot_ref(h.reshape(rows, D_MODEL), win_ref).reshape(t_new, n_seq, D_IN)
    xa = proj[:, :, 0:D_A]
    ca = proj[:, :, D_A:2 * D_A]
    ba = proj[:, :, 2 * D_A:3 * D_A]
    a_b = proj[:, :, 3 * D_A:3 * D_A + D_B]
    g_b = proj[:, :, 3 * D_A + D_B:3 * D_A + 2 * D_B]
    p_c = proj[:, :, 3 * D_A + 2 * D_B:]

    def history(state_ref, new):
        return [state_ref[j] for j in range(state_ref.shape[0])] + [new[t] for t in range(t_new)]

    def conv(seq, w_ref, taps):
        outs = []
        for t in range(t_new):
            acc = w_ref[0:1, :] * seq[t]
            for k in range(1, taps):
                acc = acc + w_ref[k:k + 1, :] * seq[t + k]
            outs.append(acc)
        return jnp.stack(outs)

    def emit_state(out_ref, seq):
        keep = out_ref.shape[0]
        for j in range(keep):
            out_ref[j] = seq[len(seq) - keep + j]

    seq_a = history(sa_ref, ca * xa)
    ya = ba * conv(seq_a, caw_ref, CONV_A)
    emit_state(na_ref, seq_a)

    seq_b = history(sb_ref, a_b * jax.nn.sigmoid(g_b))
    vb = conv(seq_b, cbw_ref, CONV_B) + cbb_ref[...]
    yb = _silu(_layer_norm(vb, lng_ref[...], lnb_ref[...]))
    emit_state(nb_ref, seq_b)

    seq_c = history(sc_ref, p_c)
    pooled = []
    for t in range(t_new):
        cur = POOL_BUF + t
        run = seq_c[cur]
        sums, counts = [], []
        for j in range(1, POOL_WINDOWS[-1]):
            run = run + seq_c[cur - j]
            if j + 1 in POOL_WINDOWS:
                sums.append(run)
                counts.append(float(min(PAST_LEN + t + 1, j + 1)))
        pooled.append(_pool_select(sums, counts, seq_c[cur]))
    d = jnp.stack(pooled)
    yc = _dot_ref(d.reshape(rows, D_C), pw_ref).reshape(t_new, n_seq, D_C) * ps_ref[...]
    emit_state(nc_ref, seq_c)

    mix = jnp.concatenate([ya, yb, yc], axis=-1).reshape(rows, D_MODEL)
    xmid = x + g_m * _dot_ref(mix, wout_ref).reshape(t_new, n_seq, D_MODEL)
    xmid_ref[...] = xmid

    h2 = _rms_mod(xmid, nfg_ref[...], sc_f, sh_f).reshape(rows, D_MODEL)
    h2_hi = h2.astype(BF16)
    h2_ref[...] = h2_hi
    lt_ref[...] = _router_logits_t(h2, h2_hi, wrh_ref[...], wrl_ref[...])


def _mixer_sample(x, prev, mod, states, lw):
    t_new, n_seq, _ = x.shape
    rows = t_new * n_seq
    args = [x]
    if prev is not None:
        args += [prev[0], prev[1]]
    args += [mod, *states, lw["norm_mix_g"], lw["w_in"], lw["conv_a_w"], lw["conv_b_w"],
             lw["conv_b_bias"], lw["ln_b_g"], lw["ln_b_b"], lw["pool_w"], lw["pool_scale"],
             lw["w_out"], lw["norm_ffn_g"], lw["wr_hi"], lw["wr_lo"]]
    out_shape = (
        jax.ShapeDtypeStruct((t_new, n_seq, D_MODEL), F32),
        jax.ShapeDtypeStruct((rows, D_MODEL), BF16),
        jax.ShapeDtypeStruct((N_EXPERTS, rows), F32),
        jax.ShapeDtypeStruct((CONV_A - 1, n_seq, D_A), F32),
        jax.ShapeDtypeStruct((CONV_B - 1, n_seq, D_B), F32),
        jax.ShapeDtypeStruct((POOL_BUF, n_seq, D_C), F32),
    )
    return pl.pallas_call(
        functools.partial(_mixer_sample_kernel, prev is not None),
        grid=(1,),
        in_specs=[_full(a.shape) for a in args],
        out_specs=tuple(_full(o.shape) for o in out_shape),
        out_shape=out_shape,
        compiler_params=pltpu.CompilerParams(
            dimension_semantics=("arbitrary",), vmem_limit_bytes=VMEM_LIMIT_BYTES),
        name="mixer_sample",
    )(*args)


def _route(lt, bias):
    n_tok = lt.shape[1]
    scores = jax.nn.sigmoid(lt)
    biased = scores + bias
    sj = [scores[GROUP_SIZE * j:GROUP_SIZE * (j + 1)] for j in range(GROUP_SIZE)]
    bj = [biased[GROUP_SIZE * j:GROUP_SIZE * (j + 1)] for j in range(GROUP_SIZE)]

    m1 = bj[0]
    m2 = jnp.full_like(m1, -jnp.inf)
    for j in range(1, GROUP_SIZE):
        m2 = jnp.maximum(m2, jnp.minimum(m1, bj[j]))
        m1 = jnp.maximum(m1, bj[j])
    gscore = m1 + m2

    gidx = lax.broadcasted_iota(jnp.int32, (N_EXPERT_GROUPS, n_tok), 0)
    lower = [None] + [jnp.where(gidx >= k, gidx - k, gidx - k + N_EXPERT_GROUPS) < gidx
                      for k in range(1, N_EXPERT_GROUPS)]

    def beats(other, mine, tie_wins):
        return (other > mine) | ((other == mine) & tie_wins)

    grank = jnp.zeros((N_EXPERT_GROUPS, n_tok), jnp.int32)
    for k in range(1, N_EXPERT_GROUPS):
        grank = grank + beats(pltpu.roll(gscore, k, 0), gscore, lower[k]).astype(jnp.int32)
    gsel = grank < TOPK_GROUPS

    def over_groups(op, v):
        for k in (4, 2, 1):
            v = op(v, pltpu.roll(v, k, 0))
        return v

    mj = [jnp.where(gsel, b, -jnp.inf) for b in bj]
    eid = [(gidx * GROUP_SIZE + j).astype(F32) for j in range(GROUP_SIZE)]
    sel = [jnp.zeros((N_EXPERT_GROUPS, n_tok), F32) for _ in range(GROUP_SIZE)]
    for _ in range(TOP_K):
        top = over_groups(jnp.maximum, functools.reduce(jnp.maximum, mj))
        first = over_groups(jnp.minimum, functools.reduce(
            jnp.minimum,
            [jnp.where(mj[j] == top, eid[j], float(N_EXPERTS)) for j in range(GROUP_SIZE)]))
        for j in range(GROUP_SIZE):
            hit = eid[j] == first
            sel[j] = jnp.where(hit, 1.0, sel[j])
            mj[j] = jnp.where(hit, -jnp.inf, mj[j])

    picked = [sel[j] * sj[j] for j in range(GROUP_SIZE)]
    tot = picked[0]
    for j in range(1, GROUP_SIZE):
        tot = tot + picked[j]
    denom = jnp.sum(tot, axis=0, keepdims=True)
    gates = [picked[j] / denom * ROUTED_SCALE for j in range(GROUP_SIZE)]
    return jnp.concatenate(sel, axis=0), jnp.concatenate(gates, axis=0)


def _moe_kernel(n_blocks, h2_ref, lt_ref, rb_ref, wg_ref, wu_ref, wd_ref, wsg_ref, wsu_ref,
                wsd_ref, o_ref, rank_ref, gate_ref, p_ref, xs_ref, ys_ref, gs_ref, codes_ref):
    q = pl.program_id(1)
    seg = MOE_CAP
    blk = MOE_BLOCK

    @pl.when(q == 0)
    def _():
        sel, gates = _route(lt_ref[...], rb_ref[...])
        gate_ref[...] = gates
        before = jnp.where(lax.broadcasted_iota(jnp.int32, (blk, blk), 0)
                           < lax.broadcasted_iota(jnp.int32, (blk, blk), 1), 1.0, 0.0).astype(BF16)
        passes = []
        for b in range(n_blocks):
            sb = sel[:, b * blk:(b + 1) * blk]
            cnt = jnp.dot(sb.astype(BF16), before, preferred_element_type=F32)
            rank_ref[:, b * blk:(b + 1) * blk] = jnp.where(sb > 0.5, cnt, -1.0)
            total = jnp.broadcast_to(jnp.max((cnt + 1.0) * sb, axis=1, keepdims=True),
                                     (N_EXPERTS, LANES))
            passes.append(((total + (seg - 1)) * (1.0 / seg)).astype(jnp.int32))
        row = lax.broadcasted_iota(jnp.int32, (N_EXPERT_GROUPS, LANES), 0) % MOE_SLOTS
        field = functools.reduce(lambda a, s: jnp.where(row == s, CODE_RADIX ** s, a),
                                 range(1, MOE_SLOTS), jnp.ones_like(row))

        def over_slots(op, v):
            v = op(v, pltpu.roll(v, 1, 0))
            return op(v, pltpu.roll(v, 2, 0))

        for j in range(GROUP_SIZE):
            slab = [p[N_EXPERT_GROUPS * j:N_EXPERT_GROUPS * (j + 1)] for p in passes]
            slot_code = over_slots(jnp.add, functools.reduce(jnp.maximum, slab) * field)
            blk_code = functools.reduce(
                jnp.add, [over_slots(jnp.maximum, slab[b]) * CODE_RADIX ** b for b in range(n_blocks)])
            for i in range(N_EXPERT_GROUPS // MOE_SLOTS):
                last = MOE_SLOTS * (i + 1) - 1
                step = (N_EXPERT_GROUPS // MOE_SLOTS) * j + i
                codes_ref[step] = jnp.max(slot_code[last:last + 1])
                codes_ref[N_EXPERTS // MOE_SLOTS + step] = jnp.max(blk_code[last:last + 1])
        x = h2_ref[...]
        hid = _silu(jnp.dot(x, wsg_ref[...], preferred_element_type=F32)) * jnp.dot(
            x, wsu_ref[...], preferred_element_type=F32)
        o_ref[...] = jnp.dot(hid.astype(BF16), wsd_ref[...], preferred_element_type=F32)

    r0 = q * MOE_SLOTS
    pos = lax.broadcasted_iota(jnp.int32, (seg, blk), 0).astype(F32)

    def passes_of(code, i):
        return lax.shift_right_logical(code, 3 * i) & (CODE_RADIX - 1)

    slot_passes = [passes_of(codes_ref[q], s) for s in range(MOE_SLOTS)]
    blk_passes = [passes_of(codes_ref[N_EXPERTS // MOE_SLOTS + q], b) for b in range(n_blocks)]

    def gather(b, p):
        base = (p * seg).astype(F32) if not isinstance(p, int) else float(p * seg)
        rows = []
        for s in range(MOE_SLOTS):
            rrow = rank_ref[pl.ds(r0 + s, 1), b * blk:(b + 1) * blk]
            grow = gate_ref[pl.ds(r0 + s, 1), b * blk:(b + 1) * blk]
            match = (rrow - base) == pos
            rows.append(jnp.where(match, 1.0, 0.0).astype(BF16))
            gs_ref[s, b] = jnp.sum(jnp.where(match, grow, 0.0), axis=1, keepdims=True)
        onehot = jnp.concatenate(rows, axis=0)
        p_ref[b] = onehot
        xs = jnp.dot(onehot, h2_ref[b * blk:(b + 1) * blk, :],
                     preferred_element_type=F32).astype(BF16)
        for s in range(MOE_SLOTS):
            xs_ref[s, b] = xs[s * seg:(s + 1) * seg]

    def expert_mlp(s):
        lhs = xs_ref[s].reshape(n_blocks * seg, D_MODEL)
        hid = _silu(jnp.dot(lhs, wg_ref[s, 0].astype(BF16), preferred_element_type=F32)) * jnp.dot(
            lhs, wu_ref[s, 0].astype(BF16), preferred_element_type=F32)
        y = jnp.dot(hid.astype(BF16), wd_ref[s, 0].astype(BF16), preferred_element_type=F32)
        y = gs_ref[s].reshape(n_blocks * seg, 1) * y
        ys_ref[s] = y.astype(BF16).reshape(n_blocks, seg, D_MODEL)

    def combine(b):
        yb = jnp.concatenate([ys_ref[s, b] for s in range(MOE_SLOTS)], axis=0)
        o_ref[b * blk:(b + 1) * blk, :] += lax.dot_general(
            p_ref[b], yb, (((0,), (0,)), ((), ())), preferred_element_type=F32)

    for b in range(n_blocks):
        gather(b, 0)
    for s in range(MOE_SLOTS):
        expert_mlp(s)
    for b in range(n_blocks):
        combine(b)

    def overflow_pass(p, carry):
        for b in range(n_blocks):
            pl.when(blk_passes[b] > p)(functools.partial(gather, b, p))
        for s in range(MOE_SLOTS):
            pl.when(slot_passes[s] > p)(functools.partial(expert_mlp, s))
        for b in range(n_blocks):
            pl.when(blk_passes[b] > p)(functools.partial(combine, b))
        return carry

    lax.fori_loop(1, functools.reduce(jnp.maximum, slot_passes), overflow_pass, 0)


def _moe(h2, lt, lw):
    n_tok = h2.shape[0]
    sup = min(MOE_SUPER, n_tok)
    n_blocks = sup // MOE_BLOCK
    n_steps = N_EXPERTS // MOE_SLOTS
    half = N_EXPERT_GROUPS // MOE_SLOTS
    layer = lw["layer"]
    assert n_blocks * 3 <= 30, "block pass counts are packed as 3-bit fields of one int32"
    wspec = lambda a, b: pl.BlockSpec(
        (None, MOE_SLOTS, 1, a, b), lambda t, q: (layer, q % half, q // half, 0, 0))
    return pl.pallas_call(
        functools.partial(_moe_kernel, n_blocks),
        grid=(n_tok // sup, n_steps),
        in_specs=[
            pl.BlockSpec((sup, D_MODEL), lambda t, q: (t, 0)),
            pl.BlockSpec((N_EXPERTS, sup), lambda t, q: (0, t)),
            _resident((N_EXPERTS, 1)),
            wspec(D_MODEL, D_EXPERT), wspec(D_MODEL, D_EXPERT), wspec(D_EXPERT, D_MODEL),
            _resident(lw["ws_gate"].shape), _resident(lw["ws_up"].shape),
            _resident(lw["ws_down"].shape),
        ],
        out_specs=pl.BlockSpec((sup, D_MODEL), lambda t, q: (t, 0), pipeline_mode=pl.Buffered(1)),
        out_shape=jax.ShapeDtypeStruct((n_tok, D_MODEL), F32),
        scratch_shapes=[
            pltpu.VMEM((N_EXPERTS, sup), F32),
            pltpu.VMEM((N_EXPERTS, sup), F32),
            pltpu.VMEM((n_blocks, MOE_SLOTS * MOE_CAP, MOE_BLOCK), BF16),
            pltpu.VMEM((MOE_SLOTS, n_blocks, MOE_CAP, D_MODEL), BF16),
            pltpu.VMEM((MOE_SLOTS, n_blocks, MOE_CAP, D_MODEL), BF16),
            pltpu.VMEM((MOE_SLOTS, n_blocks, MOE_CAP, 1), F32),
            pltpu.SMEM((2 * n_steps,), jnp.int32),
        ],
        compiler_params=pltpu.CompilerParams(
            dimension_semantics=("arbitrary", "arbitrary"), vmem_limit_bytes=MOE_VMEM_LIMIT_BYTES),
        name="moe",
    )(h2, lt, lw["router_bias"], lw["w_gate"], lw["w_up"], lw["w_down"],
      lw["ws_gate"], lw["ws_up"], lw["ws_down"])


def _final_kernel(x_ref, f_ref, mod_ref, g_ref, o_ref):
    gate = _mod_parts(mod_ref[...])[5]
    if x_ref.ndim == 3 and mod_ref.ndim == 2:
        gate = gate[None]
    x = x_ref[...] + gate * f_ref[...]
    ms = jnp.mean(x * x, axis=-1, keepdims=True)
    o_ref[...] = (x * lax.rsqrt(ms + EPS)) * g_ref[...]


def _final_prompt(xmid, ffn, mod, g):
    nb, seq, _ = xmid.shape
    tile = min(MIX_TILE, seq)
    tok = pl.BlockSpec((1, tile, D_MODEL), lambda b, s: (b, s, 0))
    return pl.pallas_call(
        _final_kernel,
        grid=(nb, seq // tile),
        in_specs=[tok, tok, pl.BlockSpec((1, 1, N_MOD * D_MODEL), lambda b, s: (b, 0, 0)),
                  _full(g.shape)],
        out_specs=tok,
        out_shape=jax.ShapeDtypeStruct(xmid.shape, F32),
        compiler_params=pltpu.CompilerParams(
            dimension_semantics=("arbitrary", "arbitrary"), vmem_limit_bytes=VMEM_LIMIT_BYTES),
        name="final_prompt",
    )(xmid, ffn, mod, g)


def _final_sample(xmid, ffn, mod, g):
    return pl.pallas_call(
        _final_kernel,
        grid=(1,),
        in_specs=[_full(xmid.shape), _full(ffn.shape), _full(mod.shape), _full(g.shape)],
        out_specs=_full(xmid.shape),
        out_shape=jax.ShapeDtypeStruct(xmid.shape, F32),
        compiler_params=pltpu.CompilerParams(
            dimension_semantics=("arbitrary",), vmem_limit_bytes=VMEM_LIMIT_BYTES),
        name="final_sample",
    )(xmid, ffn, mod, g)


def _slot_order(a, axis):
    shape = a.shape
    a = a.reshape(shape[:axis] + (N_EXPERT_GROUPS, GROUP_SIZE) + shape[axis + 1:])
    return jnp.swapaxes(a, axis, axis + 1).reshape(shape)


def _layer_weights(l, w_in, norm_mix_g, conv_a_w, conv_b_w, conv_b_bias, ln_b_g, ln_b_b, pool_w,
                   pool_scale, w_out, norm_ffn_g, w_router, router_bias, w_gate, w_up, w_down,
                   ws_gate, ws_up, ws_down):
    row = lambda v: v[l].reshape(1, -1)
    wr = _slot_order(w_router[l], 1).T
    wr_hi = wr.astype(BF16)
    blockdiag = jax.scipy.linalg.block_diag(*[pool_w[l, g] for g in range(pool_w.shape[1])])
    precise = l + 1 < w_in.shape[0]
    def stacked(w):
        hi, lo = _split(w)
        return jnp.concatenate([hi, hi, lo], axis=0) if precise else hi
    return dict(
        norm_mix_g=row(norm_mix_g), w_in=stacked(w_in[l]), conv_a_w=conv_a_w[l],
        conv_b_w=conv_b_w[l], conv_b_bias=row(conv_b_bias), ln_b_g=row(ln_b_g), ln_b_b=row(ln_b_b),
        pool_w=stacked(blockdiag), pool_scale=row(pool_scale), w_out=stacked(w_out[l]),
        norm_ffn_g=row(norm_ffn_g), wr_hi=wr_hi, wr_lo=(wr - wr_hi.astype(F32)).astype(BF16),
        router_bias=_slot_order(router_bias[l], 0).reshape(N_EXPERTS, 1),
        layer=l, w_gate=w_gate, w_up=w_up, w_down=w_down,
        ws_gate=ws_gate[l].astype(BF16), ws_up=ws_up[l].astype(BF16), ws_down=ws_down[l].astype(BF16),
    )


def kernel(x_prompt, x_sample, c_prompt, c_sample, state_conv_a, state_conv_b, state_pool, w_ada, b_ada, norm_mix_g, w_in, conv_a_w, conv_b_w, conv_b_bias, ln_b_g, ln_b_b, pool_w, pool_scale, w_out, norm_ffn_g, w_router, router_bias, w_gate, w_up, w_down, ws_gate, ws_up, ws_down, final_norm_g):
    depth = w_ada.shape[0]
    n_p, seq, _ = x_prompt.shape
    n_s, t_new, _ = x_sample.shape

    mod = _ada(jnp.concatenate([c_prompt, c_sample], axis=0), w_ada, b_ada)
    mod_p = mod[:, :n_p].reshape(depth, n_p, 1, N_MOD * D_MODEL)
    mod_s = mod[:, n_p:]

    xp = x_prompt
    xs = jnp.swapaxes(x_sample, 0, 1)
    time_major = lambda st: jnp.swapaxes(st, 1, 2)
    st_a, st_b, st_c = time_major(state_conv_a), time_major(state_conv_b), time_major(state_pool)

    experts = lambda w: w.reshape((depth, N_EXPERT_GROUPS, GROUP_SIZE) + w.shape[2:])
    w_gate, w_up, w_down = experts(w_gate), experts(w_up), experts(w_down)

    prev_p = prev_s = None
    new_p, new_s = [], []
    for l in range(depth):
        lw = _layer_weights(l, w_in, norm_mix_g, conv_a_w, conv_b_w, conv_b_bias, ln_b_g, ln_b_b,
                            pool_w, pool_scale, w_out, norm_ffn_g, w_router, router_bias,
                            w_gate, w_up, w_down, ws_gate, ws_up, ws_down)
        xp, h2p, ltp, na, nb, nc = _mixer_prompt(xp, prev_p, mod_p[l], lw)
        new_p.append((na, nb, nc))
        ffn_p = _moe(h2p.reshape(n_p * seq, D_MODEL), ltp, lw).reshape(n_p, seq, D_MODEL)
        prev_p = (ffn_p, mod_p[l])

        xs, h2s, lts, na, nb, nc = _mixer_sample(xs, prev_s, mod_s[l], (st_a[l], st_b[l], st_c[l]), lw)
        new_s.append((na, nb, nc))
        ffn_s = _moe(h2s, lts, lw).reshape(t_new, n_s, D_MODEL)
        prev_s = (ffn_s, mod_s[l])

    g = final_norm_g.reshape(1, D_MODEL)
    y_prompt = _final_prompt(xp, prev_p[0], prev_p[1], g)
    y_sample = jnp.swapaxes(_final_sample(xs, prev_s[0], prev_s[1], g), 0, 1)

    stack = lambda items, i: jnp.stack([it[i] for it in items])
    batch_major = lambda a: jnp.swapaxes(a, 1, 2)
    return (y_prompt, y_sample,
            stack(new_p, 0), stack(new_p, 1), stack(new_p, 2),
            batch_major(stack(new_s, 0)), batch_major(stack(new_s, 1)), batch_major(stack(new_s, 2)))
```
